```python
import jax, jax.numpy as jnp
from jax import lax
import numpy as np

D_MODEL = 1024
BATCH = 8
SEQ = 8192
DEPTH = 4

N_META = 16
EPS = 1e-6
NEG_INF = -1e30
FOX_HEADS = 8
FOX_HEAD_DIM = 64
FOX_BLOCK = 128
FOX_W = FOX_HEADS * FOX_HEAD_DIM
GDN_HEADS = 8
GDN_HEAD_DIM = 128
GDN_CHUNK = 64
GDN_CONV = 4
GDN_W = GDN_HEADS * GDN_HEAD_DIM
N_BRANCH = 2
D_FF = 2816
FFN_CONV = 3
IN_SIZES = [FOX_W, FOX_W, FOX_W, FOX_HEADS,
            GDN_W, GDN_W, GDN_W, GDN_HEADS, GDN_HEADS,
            GDN_W,
            N_BRANCH * D_MODEL]
D_IN = int(sum(IN_SIZES))
IN_SPLIT = [int(c) for c in np.cumsum(IN_SIZES)[:-1]]

kernel_name = "hybrid_fox_gdn_convffn_trunk"


def rmsnorm(x, g):
    xf = x.astype(jnp.float32)
    y = xf * lax.rsqrt(jnp.mean(xf * xf, axis=-1, keepdims=True) + EPS)
    return (y * g.astype(jnp.float32)).astype(x.dtype)


def l2norm(x):
    return x * lax.rsqrt(jnp.sum(x * x, axis=-1, keepdims=True) + EPS)


def causal_dwconv(x, w):
    K, C = w.shape
    return lax.conv_general_dilated(x, w[:, None, :].astype(x.dtype), window_strides=(1,),
                                    padding=[(K - 1, 0)], dimension_numbers=('NWC', 'WIO', 'NWC'),
                                    feature_group_count=C)


def fox_attention(q, k, v, log_f):
    B_, L, H, Dh = q.shape
    pad = (-L) % FOX_BLOCK
    Lp = L + pad
    nb = Lp // FOX_BLOCK
    p4 = ((0, 0), (pad, 0), (0, 0), (0, 0))
    qp, kp, vp = jnp.pad(q, p4), jnp.pad(k, p4), jnp.pad(v, p4)
    F = jnp.cumsum(jnp.pad(log_f, ((0, 0), (pad, 0), (0, 0))), axis=1)
    Fk = F.transpose(0, 2, 1)
    kpos = jnp.arange(Lp)
    key_valid = kpos >= pad
    qb = qp.reshape(B_, nb, FOX_BLOCK, H, Dh).transpose(1, 0, 2, 3, 4)
    Fq = Fk.reshape(B_, H, nb, FOX_BLOCK).transpose(2, 0, 1, 3)
    scale = Dh ** -0.5

    def block(args):
        i, q_blk, F_blk = args
        qpos = i * FOX_BLOCK + jnp.arange(FOX_BLOCK)
        s = jnp.einsum('bqhd,bkhd->bhqk', q_blk, kp, preferred_element_type=jnp.float32) * scale
        s = s + F_blk[..., :, None] - Fk[:, :, None, :]
        mask = (kpos[None, :] <= qpos[:, None]) & key_valid[None, :]
        p = jax.nn.softmax(jnp.where(mask, s, NEG_INF), axis=-1)
        return jnp.einsum('bhqk,bkhd->bqhd', p.astype(vp.dtype), vp)

    out = lax.map(block, (jnp.arange(nb), qb, Fq))
    return out.transpose(1, 0, 2, 3, 4).reshape(B_, Lp, H, Dh)[:, pad:]


def gated_delta_rule(q, k, v, beta, g):
    B_, L, H, Dk = q.shape
    Dv = v.shape[-1]
    C = GDN_CHUNK
    pad = (-L) % C
    Lp = L + pad
    N = Lp // C
    p4 = ((0, 0), (pad, 0), (0, 0), (0, 0))
    p3 = ((0, 0), (pad, 0), (0, 0))

    def chunks4(t):
        return jnp.pad(t, p4).reshape(B_, N, C, H, t.shape[-1]).transpose(0, 3, 1, 2, 4)

    def chunks3(t):
        return jnp.pad(t, p3).reshape(B_, N, C, H).transpose(0, 3, 1, 2)

    qc, kc, vc = chunks4(q), chunks4(k), chunks4(v)
    bc, gc = chunks3(beta), chunks3(g)
    G = jnp.cumsum(gc, axis=-1)
    idx = jnp.arange(C)
    incl = idx[:, None] >= idx[None, :]
    strict = idx[:, None] > idx[None, :]
    diff = G[..., :, None] - G[..., None, :]
    decay_incl = jnp.exp(jnp.where(incl, diff, -jnp.inf))
    decay_strict = jnp.where(strict, decay_incl, 0.0)
    kb = kc * bc[..., None]
    M = jnp.einsum('bhncd,bhnsd->bhncs', kb, kc) * decay_strict
    A = M + jnp.eye(C, dtype=M.dtype)
    u_hat = lax.linalg.triangular_solve(A, vc * bc[..., None], left_side=True, lower=True, unit_diagonal=True)
    w = lax.linalg.triangular_solve(A, kb * jnp.exp(G)[..., None], left_side=True, lower=True, unit_diagonal=True)
    qg = qc * jnp.exp(G)[..., None]
    aqk = jnp.einsum('bhncd,bhnsd->bhncs', qc, kc) * decay_incl
    kd = kc * jnp.exp(G[..., -1:] - G)[..., None]
    gC = jnp.exp(G[..., -1])

    def to_front(t):
        return jnp.moveaxis(t, 2, 0)

    def step(S, xs):
        u_c, w_c, qg_c, a_c, kd_c, g_c = xs
        U = u_c - jnp.einsum('bhck,bhkv->bhcv', w_c, S)
        o = jnp.einsum('bhck,bhkv->bhcv', qg_c, S) + jnp.einsum('bhcs,bhsv->bhcv', a_c, U)
        S = S * g_c[..., None, None] + jnp.einsum('bhck,bhcv->bhkv', kd_c, U)
        return S, o

    S0 = jnp.zeros((B_, H, Dk, Dv), jnp.float32)
    _, o = lax.scan(step, S0, (to_front(u_hat), to_front(w), to_front(qg), to_front(aqk), to_front(kd), to_front(gC)))
    return o.transpose(1, 0, 3, 2, 4).reshape(B_, Lp, H, Dv)[:, pad:]


def setup_inputs(seed: int = 0) -> dict:
    key = jax.random.key(seed)
    ks = jax.random.split(key, 20)
    f32 = jnp.float32

    def nrm(k, shape, scale):
        return jax.random.normal(k, shape, f32) * scale

    dt = jnp.exp(jax.random.uniform(ks[7], (DEPTH, GDN_HEADS), f32, np.log(1e-3), np.log(1e-1)))
    return {
        "x": nrm(ks[0], (BATCH, SEQ, D_MODEL), 1.0),
        "meta_tokens": nrm(ks[1], (N_META, D_MODEL), 1.0),
        "norm1_g": 1.0 + nrm(ks[2], (DEPTH, D_MODEL), 0.02),
        "w_in": nrm(ks[3], (DEPTH, D_MODEL, D_IN), D_MODEL ** -0.5),
        "fox_f_bias": 2.0 + nrm(ks[4], (DEPTH, FOX_HEADS), 0.5),
        "fox_q_norm_g": 1.0 + nrm(ks[5], (DEPTH, FOX_HEAD_DIM), 0.02),
        "fox_k_norm_g": 1.0 + nrm(ks[6], (DEPTH, FOX_HEAD_DIM), 0.02),
        "gdn_conv_w": nrm(ks[8], (DEPTH, GDN_CONV, 3 * GDN_W), GDN_CONV ** -0.5),
        "gdn_a_log": jnp.log(jax.random.uniform(ks[9], (DEPTH, GDN_HEADS), f32, 1.0, 16.0)),
        "gdn_dt_bias": jnp.log(jnp.expm1(dt)),
        "gdn_norm_g": 1.0 + nrm(ks[10], (DEPTH, GDN_HEAD_DIM), 0.02),
        "w_branch_a": nrm(ks[11], (DEPTH, FOX_W, D_MODEL), FOX_W ** -0.5),
        "w_branch_b": nrm(ks[12], (DEPTH, GDN_W, D_MODEL), GDN_W ** -0.5),
        "w_out": nrm(ks[13], (DEPTH, D_MODEL, D_MODEL), D_MODEL ** -0.5),
        "norm2_g": 1.0 + nrm(ks[14], (DEPTH, D_MODEL), 0.02),
        "w_up": nrm(ks[15], (DEPTH, D_MODEL, 2 * D_FF), D_MODEL ** -0.5),
        "ffn_conv_w": nrm(ks[16], (DEPTH, FFN_CONV, 2 * D_FF), FFN_CONV ** -0.5),
        "w_down": nrm(ks[17], (DEPTH, D_FF, D_MODEL), D_FF ** -0.5),
    }


def reference(x, meta_tokens, norm1_g, w_in, fox_f_bias, fox_q_norm_g, fox_k_norm_g, gdn_conv_w,
              gdn_a_log, gdn_dt_bias, gdn_norm_g, w_branch_a, w_branch_b, w_out, norm2_g, w_up,
              ffn_conv_w, w_down):
    B_, S_, D = x.shape
    meta = jnp.broadcast_to(meta_tokens.astype(x.dtype)[None], (B_, N_META, D))
    h_res = jnp.concatenate([meta, x], axis=1)
    L = h_res.shape[1]
    f32 = jnp.float32
    for l in range(DEPTH):
        h = rmsnorm(h_res, norm1_g[l])
        proj = h @ w_in[l]
        (fq, fk, fv, f_logit, gq, gk, gv, b_logit, a_logit, gz, gate_logit) = jnp.split(proj, IN_SPLIT, axis=-1)

        fq = rmsnorm(fq.reshape(B_, L, FOX_HEADS, FOX_HEAD_DIM), fox_q_norm_g[l])
        fk = rmsnorm(fk.reshape(B_, L, FOX_HEADS, FOX_HEAD_DIM), fox_k_norm_g[l])
        fv = fv.reshape(B_, L, FOX_HEADS, FOX_HEAD_DIM)
        log_f = jax.nn.log_sigmoid(f_logit.astype(f32) + fox_f_bias[l].astype(f32))
        y_a = fox_attention(fq, fk, fv, log_f).reshape(B_, L, FOX_W) @ w_branch_a[l]

        qkv = jax.nn.silu(causal_dwconv(jnp.concatenate([gq, gk, gv], axis=-1), gdn_conv_w[l]))
        gq, gk, gv = jnp.split(qkv.astype(f32), 3, axis=-1)
        gq = l2norm(gq.reshape(B_, L, GDN_HEADS, GDN_HEAD_DIM)) * (GDN_HEAD_DIM ** -0.5)
        gk = l2norm(gk.reshape(B_, L, GDN_HEADS, GDN_HEAD_DIM))
        gv = gv.reshape(B_, L, GDN_HEADS, GDN_HEAD_DIM)
        beta = jax.nn.sigmoid(b_logit.astype(f32))
        g = -jnp.exp(gdn_a_log[l].astype(f32)) * jax.nn.softplus(a_logit.astype(f32) + gdn_dt_bias[l].astype(f32))
        o_b = gated_delta_rule(gq, gk, gv, beta, g)
        o_b = rmsnorm(o_b, gdn_norm_g[l]).astype(x.dtype) * jax.nn.silu(gz.reshape(B_, L, GDN_HEADS, GDN_HEAD_DIM))
        y_b = o_b.reshape(B_, L, GDN_W) @ w_branch_b[l]

        gates = jax.nn.sigmoid(gate_logit).reshape(B_, L, N_BRANCH, D)
        mixed = gates[:, :, 0] * y_a + gates[:, :, 1] * y_b
        h_res = h_res + mixed @ w_out[l]

        h = rmsnorm(h_res, norm2_g[l])
        up = causal_dwconv(h @ w_up[l], ffn_conv_w[l])
        u_gate, u_val = jnp.split(up, 2, axis=-1)
        h_res = h_res + (jax.nn.silu(u_gate) * u_val) @ w_down[l]
    return h_res[:, N_META:]
```

```python
import functools

import numpy as np
import jax
import jax.numpy as jnp
from jax import lax
from jax.experimental import pallas as pl
from jax.experimental.pallas import tpu as pltpu

D_MODEL = 1024
DEPTH = 4
N_META = 16
EPS = 1e-6
NEG_INF = -1e30
FOX_HEADS = 8
FOX_HEAD_DIM = 64
FOX_W = FOX_HEADS * FOX_HEAD_DIM
GDN_HEADS = 8
GDN_HEAD_DIM = 128
GDN_CHUNK = 64
GDN_CONV = 4
GDN_W = GDN_HEADS * GDN_HEAD_DIM
D_FF = 2816
FFN_CONV = 3
FFN_COL_CHUNK = 1408

LANES = 128
SUBLANES = 8
TOKEN_TILE = 640
VMEM_LIMIT = 56 * 1024 * 1024

_OFF_FQ = 0
_OFF_FLOGIT = 3 * FOX_W
_OFF_GQ = _OFF_FLOGIT + FOX_HEADS
_OFF_BLOGIT = _OFF_GQ + 3 * GDN_W
_OFF_ALOGIT = _OFF_BLOGIT + GDN_HEADS
_OFF_GZ = _OFF_ALOGIT + GDN_HEADS
_OFF_GATE = _OFF_GZ + GDN_W

_SM_F = 0
_SM_ONE = 24
_SM_B = 32
_SM_A = 40
_SM_MASK = 32

_AUG = FOX_HEAD_DIM

bf16 = jnp.bfloat16
f32 = jnp.float32


def _mm(a, b):
    return jnp.dot(a, b, preferred_element_type=f32)


def _mm_nt(a, b):
    return lax.dot_general(a, b, (((1,), (1,)), ((), ())), preferred_element_type=f32)


def _mm_tn(a, b):
    return lax.dot_general(a, b, (((0,), (0,)), ((), ())), preferred_element_type=f32)


def _split3(x):
    p1 = x.astype(bf16)
    r1 = x - p1.astype(f32)
    p2 = r1.astype(bf16)
    p3 = (r1 - p2.astype(f32)).astype(bf16)
    return p1, p2, p3


def _mm3(m01, x):
    p1, p2, p3 = _split3(x)
    return (_mm(m01, p3) + _mm(m01, p2)) + _mm(m01, p1)


def _sigmoid(x):
    return 1.0 / (1.0 + jnp.exp(-x))


def _silu(x):
    return x * _sigmoid(x)


def _softplus(x):
    return jnp.maximum(x, 0.0) + jnp.log1p(jnp.exp(-jnp.abs(x)))


def _rms_rows(x, g):
    ms = jnp.mean(x * x, axis=-1, keepdims=True)
    return x * lax.rsqrt(ms + EPS) * g


def _row_valid(t, T, P):
    pos = t * T + lax.broadcasted_iota(jnp.int32, (T, 1), 0)
    return pos >= P


def _fox_proj_kernel(h_ref, n1g_ref, wfox_ref, wsm_ref, smb_ref, alog_ref, qg_ref, kg_ref,
                     tri_ref, blk_ref, eq_ref, ek_ref, ev_ref, vone_ref,
                     qa_ref, ka_ref, va_ref, sm_ref, carry_ref, *, T, P):
    t = pl.program_id(1)

    @pl.when(t == 0)
    def _():
        carry_ref[...] = jnp.zeros_like(carry_ref)

    hn = _rms_rows(h_ref[0], n1g_ref[...]).astype(bf16)
    valid = _row_valid(t, T, P)
    lane = lax.broadcasted_iota(jnp.int32, (T, LANES), 1)

    ysm = _mm(hn, wsm_ref[...]) + smb_ref[...]
    logf = -_softplus(-ysm)
    logf = jnp.where(lane < _SM_ONE, jnp.where(valid, logf, 0.0), 0.0)
    F = _mm3(tri_ref[...], logf) + carry_ref[...]
    carry_ref[...] = F[T - 1:T, :]
    beta = _sigmoid(ysm)
    g = -jnp.exp(alog_ref[...]) * _softplus(ysm)
    sm_ref[0] = jnp.where(lane < _SM_B, 0.0,
                          jnp.where(lane < _SM_A, beta, jnp.where(lane < _SM_A + GDN_HEADS, g, 0.0)))

    f1 = F.astype(bf16).astype(f32)
    f2 = (F - f1).astype(bf16).astype(f32)
    f3 = (F - f1 - f2).astype(bf16).astype(f32)
    fsel = jnp.where(lane < 8, f1, jnp.where(lane < 16, f2, f3))
    key_bias = jnp.where(valid, 0.0, NEG_INF)
    pq = jnp.where(lane < _SM_ONE, fsel, jnp.where(lane < _SM_ONE + 8, 1.0, 0.0)).astype(bf16)
    pk = jnp.where(lane < _SM_ONE, -fsel,
                   jnp.where(lane < _SM_ONE + 8, 1.0,
                             jnp.where(lane < _SM_MASK + 8, key_bias, 0.0))).astype(bf16)

    yf = _mm(hn, wfox_ref[...])
    q = yf[:, :FOX_W]
    k = yf[:, FOX_W:2 * FOX_W]
    v = yf[:, 2 * FOX_W:]
    inv_d = 1.0 / FOX_HEAD_DIM
    qss = _mm((q * q).astype(bf16), blk_ref[...])
    kss = _mm((k * k).astype(bf16), blk_ref[...])
    qn = q * lax.rsqrt(qss * inv_d + EPS) * qg_ref[...] * (FOX_HEAD_DIM ** -0.5)
    kn = k * lax.rsqrt(kss * inv_d + EPS) * kg_ref[...]

    q_aug = _mm(jnp.concatenate([qn.astype(bf16), pq], axis=1), eq_ref[...])
    k_aug = _mm(jnp.concatenate([kn.astype(bf16), pk], axis=1), ek_ref[...])
    v_aug = _mm(v.astype(bf16), ev_ref[...]) + vone_ref[...]
    for h in range(FOX_HEADS):
        sl = slice(h * LANES, (h + 1) * LANES)
        qa_ref[0, h] = q_aug[:, sl].astype(bf16)
        ka_ref[0, h] = k_aug[:, sl].astype(bf16)
        va_ref[0, h] = v_aug[:, sl].astype(bf16)


def _fox_consts(T):
    tri = np.tril(np.ones((T, T), np.float32))
    blk = np.kron(np.eye(FOX_HEADS, dtype=np.float32), np.ones((FOX_HEAD_DIM, FOX_HEAD_DIM), np.float32))
    eq = np.zeros((FOX_W + LANES, FOX_HEADS * LANES), np.float32)
    ek = np.zeros_like(eq)
    ev = np.zeros((FOX_W, FOX_HEADS * LANES), np.float32)
    vone = np.zeros((1, FOX_HEADS * LANES), np.float32)
    for h in range(FOX_HEADS):
        base = h * LANES
        for d in range(FOX_HEAD_DIM):
            eq[h * FOX_HEAD_DIM + d, base + d] = 1.0
            ek[h * FOX_HEAD_DIM + d, base + d] = 1.0
        for j in range(3):
            eq[FOX_W + _SM_F + 8 * j + h, base + _AUG + j] = 1.0
            eq[FOX_W + _SM_ONE + h, base + _AUG + 3 + j] = 1.0
            ek[FOX_W + _SM_ONE + h, base + _AUG + j] = 1.0
            ek[FOX_W + _SM_F + 8 * j + h, base + _AUG + 3 + j] = 1.0
        eq[FOX_W + _SM_ONE + h, base + _AUG + 6] = 1.0
        ek[FOX_W + _SM_MASK + h, base + _AUG + 6] = 1.0
        voff, ocol = (0, FOX_HEAD_DIM) if h % 2 == 0 else (FOX_HEAD_DIM, 0)
        for d in range(FOX_HEAD_DIM):
            ev[h * FOX_HEAD_DIM + d, base + voff + d] = 1.0
        vone[0, base + ocol] = 1.0
    as_bf = lambda a: jnp.asarray(a, bf16)
    return as_bf(tri), as_bf(blk), as_bf(eq), as_bf(ek), as_bf(ev), jnp.asarray(vone, f32)


def _const_spec(shape):
    nd = len(shape)
    return pl.BlockSpec(shape, lambda *_: (0,) * nd, pipeline_mode=pl.Buffered(1))


def _fox_proj(h, n1g, wfox, wsm, smb, alog, qg, kg, consts, *, T, P):
    B, Lp, D = h.shape
    nT = Lp // T
    tri, blk, eq, ek, ev, vone = consts
    head_spec = pl.BlockSpec((1, FOX_HEADS, T, LANES), lambda b, t: (b, 0, t, 0))
    head_shape = jax.ShapeDtypeStruct((B, FOX_HEADS, Lp, LANES), bf16)
    ins = [h, n1g, wfox, wsm, smb, alog, qg, kg, tri, blk, eq, ek, ev, vone]
    in_specs = [pl.BlockSpec((1, T, D), lambda b, t: (b, t, 0))] + [_const_spec(a.shape) for a in ins[1:]]
    return pl.pallas_call(
        functools.partial(_fox_proj_kernel, T=T, P=P),
        grid=(B, nT),
        in_specs=in_specs,
        out_specs=[head_spec, head_spec, head_spec,
                   pl.BlockSpec((1, T, LANES), lambda b, t: (b, t, 0))],
        out_shape=[head_shape, head_shape, head_shape,
                   jax.ShapeDtypeStruct((B, Lp, LANES), f32)],
        scratch_shapes=[pltpu.VMEM((1, LANES), f32)],
        compiler_params=pltpu.CompilerParams(
            dimension_semantics=("parallel", "arbitrary"), vmem_limit_bytes=VMEM_LIMIT),
        name="fox_proj",
    )(*ins)


def _gdn_proj_kernel(h_ref, n1g_ref, wgdn_ref, wgz_ref, wgate_ref, qkv_ref, gz_ref, gate_ref):
    hn = _rms_rows(h_ref[0], n1g_ref[...]).astype(bf16)
    for c in range(3):
        sl = slice(c * GDN_W, (c + 1) * GDN_W)
        qkv_ref[0, :, sl] = _mm(hn, wgdn_ref[:, sl])
    gz_ref[0] = _silu(_mm(hn, wgz_ref[...])).astype(bf16)
    for c in range(2):
        sl = slice(c * D_MODEL, (c + 1) * D_MODEL)
        gate_ref[0, :, sl] = _sigmoid(_mm(hn, wgate_ref[:, sl])).astype(bf16)


def _gdn_proj(h, n1g, wgdn, wgz, wgate, *, T):
    B, Lp, D = h.shape
    tok = lambda w: pl.BlockSpec((1, T, w), lambda b, t: (b, t, 0))
    ins = [h, n1g, wgdn, wgz, wgate]
    return pl.pallas_call(
        _gdn_proj_kernel,
        grid=(B, Lp // T),
        in_specs=[tok(D)] + [_const_spec(a.shape) for a in ins[1:]],
        out_specs=[tok(3 * GDN_W), tok(GDN_W), tok(2 * D_MODEL)],
        out_shape=[jax.ShapeDtypeStruct((B, Lp, 3 * GDN_W), f32),
                   jax.ShapeDtypeStruct((B, Lp, GDN_W), bf16),
                   jax.ShapeDtypeStruct((B, Lp, 2 * D_MODEL), bf16)],
        compiler_params=pltpu.CompilerParams(
            dimension_semantics=("parallel", "parallel"), vmem_limit_bytes=VMEM_LIMIT),
        name="gdn_proj",
    )(*ins)


def _fox_attn_kernel(q_ref, k_ref, v_ref, o_ref, *, TQ):
    qi = pl.program_id(2)
    row = lax.broadcasted_iota(jnp.int32, (TQ, TQ), 0)
    col = lax.broadcasted_iota(jnp.int32, (TQ, TQ), 1)
    causal = col <= row
    lane = lax.broadcasted_iota(jnp.int32, (TQ, LANES), 1)
    normed = []
    for hh in range(2):
        q = q_ref[0, hh]

        def block(kb, m, acc, masked):
            off = pl.multiple_of(kb * TQ, TQ)
            kblk = k_ref[0, hh, pl.ds(off, TQ), :]
            vblk = v_ref[0, hh, pl.ds(off, TQ), :]
            s = _mm_nt(q, kblk)
            if masked:
                s = jnp.where(causal, s, NEG_INF)
            m_new = jnp.maximum(m, jnp.max(s, axis=-1, keepdims=True))
            p = jnp.exp(s - m_new)
            acc = jnp.exp(m - m_new) * acc + _mm(p.astype(bf16), vblk)
            return m_new, acc

        m0 = jnp.full((TQ, 1), NEG_INF, f32)
        acc0 = jnp.zeros((TQ, LANES), f32)
        m, acc = lax.fori_loop(0, qi, lambda kb, c: block(kb, c[0], c[1], False), (m0, acc0))
        m, acc = block(qi, m, acc, True)
        lcol = FOX_HEAD_DIM if hh == 0 else 0
        normed.append(acc * (1.0 / acc[:, lcol:lcol + 1]))
    o_ref[0] = jnp.where(lane < FOX_HEAD_DIM, normed[0], normed[1]).astype(bf16)


def _fox_attn(qa, ka, va, *, TQ):
    B, H, Lp, _ = qa.shape
    kv_spec = pl.BlockSpec((1, 2, Lp, LANES), lambda b, hp, qi: (b, hp, 0, 0))
    return pl.pallas_call(
        functools.partial(_fox_attn_kernel, TQ=TQ),
        grid=(B, H // 2, Lp // TQ),
        in_specs=[pl.BlockSpec((1, 2, TQ, LANES), lambda b, hp, qi: (b, hp, qi, 0)), kv_spec, kv_spec],
        out_specs=pl.BlockSpec((1, TQ, LANES), lambda b, hp, qi: (b, qi, hp)),
        out_shape=jax.ShapeDtypeStruct((B, Lp, FOX_W), bf16),
        compiler_params=pltpu.CompilerParams(
            dimension_semantics=("parallel", "parallel", "arbitrary"), vmem_limit_bytes=VMEM_LIMIT),
        name="fox_attn",
    )(qa, ka, va)


def _gdn_kernel(x_ref, sm_ref, gz_ref, cw_ref, ng_ref, tribd_ref, onebd_ref, o_ref,
                S_ref, halo_ref, xbuf, G_s, qg_s, qn_s, kn_s, kb_s, kd_s, vb_s, *, T, C):
    t = pl.program_id(1)
    H, Dh = GDN_HEADS, GDN_HEAD_DIM

    @pl.when(t == 0)
    def _():
        S_ref[...] = jnp.zeros_like(S_ref)
        halo_ref[...] = jnp.zeros_like(halo_ref)

    sm = sm_ref[0]
    lane = lax.broadcasted_iota(jnp.int32, (T, LANES), 1)
    gl = jnp.where(lane < _SM_A, 0.0, jnp.where(lane < _SM_A + H, sm, 0.0))
    G = _mm3(tribd_ref[...], gl)
    Glast = _mm3(onebd_ref[...], gl)
    G_s[...] = G
    eG = jnp.exp(G)
    eKd = jnp.exp(Glast - G)

    def conv_silu(cb):
        cs = slice(cb * Dh, (cb + 1) * Dh)
        xbuf[0:SUBLANES, :] = halo_ref[:, cs]
        xbuf[SUBLANES:SUBLANES + T, :] = x_ref[0, :, cs]
        y = cw_ref[GDN_CONV - 1:GDN_CONV, cs] * xbuf[SUBLANES:SUBLANES + T, :]
        for j in range(1, GDN_CONV):
            y = y + cw_ref[GDN_CONV - 1 - j:GDN_CONV - j, cs] * xbuf[SUBLANES - j:SUBLANES - j + T, :]
        return _silu(y)

    def l2n(a):
        return a * lax.rsqrt(jnp.sum(a * a, axis=-1, keepdims=True) + EPS)

    for h in range(H):
        qh = l2n(conv_silu(h)) * (Dh ** -0.5)
        kh = l2n(conv_silu(H + h))
        vh = conv_silu(2 * H + h)
        beta = sm[:, _SM_B + h:_SM_B + h + 1]
        eg = eG[:, _SM_A + h:_SM_A + h + 1]
        ekd = eKd[:, _SM_A + h:_SM_A + h + 1]
        kbh = kh * beta
        qn_s[h] = qh.astype(bf16)
        qg_s[h] = (qh * eg).astype(bf16)
        kn_s[h] = kh.astype(bf16)
        kb_s[h] = kbh.astype(bf16)
        kd_s[h] = (kh * ekd).astype(bf16)
        vb_s[h, :, 0:Dh] = (vh * beta).astype(bf16)
        vb_s[h, :, Dh:2 * Dh] = (kbh * eg).astype(bf16)
    halo_ref[...] = x_ref[0, T - SUBLANES:T, :]

    ri = lax.broadcasted_iota(jnp.int32, (C, C), 0)
    ci = lax.broadcasted_iota(jnp.int32, (C, C), 1)
    strict = ri > ci
    incl = ri >= ci
    eye = (ri == ci).astype(f32)
    n_double = int(np.log2(C)) - 1

    def chunk(c, carry):
        r0 = pl.multiple_of(c * C, C)
        rows = pl.ds(r0, C)
        Gc = G_s[rows, :]
        GcT = jnp.transpose(Gc)
        for h in range(H):
            gcol = Gc[:, _SM_A + h:_SM_A + h + 1]
            grow = GcT[_SM_A + h:_SM_A + h + 1, :]
            diff = gcol - grow
            dec = jnp.exp(jnp.where(incl, diff, NEG_INF))
            kn = kn_s[h, rows, :]
            M = jnp.where(strict, _mm_nt(kb_s[h, rows, :], kn) * dec, 0.0)
            Nk = -M
            Tm = eye + Nk
            for _ in range(n_double):
                Nb = Nk.astype(bf16)
                Nk = _mm(Nb, Nb)
                Tm = Tm + _mm(Tm.astype(bf16), Nk.astype(bf16))
            uw = _mm(Tm.astype(bf16), vb_s[h, rows, :])
            S = S_ref[h]
            Sb = S.astype(bf16)
            U = uw[:, 0:Dh] - _mm(uw[:, Dh:2 * Dh].astype(bf16), Sb)
            Ub = U.astype(bf16)
            aqk = jnp.where(incl, _mm_nt(qn_s[h, rows, :], kn) * dec, 0.0)
            o = _mm(qg_s[h, rows, :], Sb) + _mm(aqk.astype(bf16), Ub)
            glast = Gc[C - 1:C, _SM_A + h:_SM_A + h + 1]
            S_ref[h] = S * jnp.exp(glast) + _mm_tn(kd_s[h, rows, :], Ub)
            cs = slice(h * Dh, (h + 1) * Dh)
            on = _rms_rows(o, ng_ref[:, cs])
            o_ref[0, rows, cs] = (on * gz_ref[0, rows, cs].astype(f32)).astype(bf16)
        return carry

    lax.fori_loop(0, T // C, chunk, 0)


def _gdn_consts(T, C):
    n = T // C
    tribd = np.kron(np.eye(n, dtype=np.float32), np.tril(np.ones((C, C), np.float32)))
    onebd = np.kron(np.eye(n, dtype=np.float32), np.ones((C, C), np.float32))
    return jnp.asarray(tribd, bf16), jnp.asarray(onebd, bf16)


def _gdn(qkv, sm, gz, cw, ng, consts, *, T, C):
    B, Lp, _ = qkv.shape
    H, Dh = GDN_HEADS, GDN_HEAD_DIM
    tribd, onebd = consts
    tok = lambda w: pl.BlockSpec((1, T, w), lambda b, t: (b, t, 0))
    ins = [qkv, sm, gz, cw, ng, tribd, onebd]
    head_bf = pltpu.VMEM((H, T, Dh), bf16)
    return pl.pallas_call(
        functools.partial(_gdn_kernel, T=T, C=C),
        grid=(B, Lp // T),
        in_specs=[tok(3 * GDN_W), tok(LANES), tok(GDN_W)] + [_const_spec(a.shape) for a in ins[3:]],
        out_specs=tok(GDN_W),
        out_shape=jax.ShapeDtypeStruct((B, Lp, GDN_W), bf16),
        scratch_shapes=[pltpu.VMEM((H, Dh, Dh), f32),
                        pltpu.VMEM((SUBLANES, 3 * GDN_W), f32),
                        pltpu.VMEM((SUBLANES + T, Dh), f32),
                        pltpu.VMEM((T, LANES), f32),
                        head_bf, head_bf, head_bf, head_bf, head_bf,
                        pltpu.VMEM((H, T, 2 * Dh), bf16)],
        compiler_params=pltpu.CompilerParams(
            dimension_semantics=("parallel", "arbitrary"), vmem_limit_bytes=VMEM_LIMIT),
        name="gdn",
    )(*ins)


def _merge_kernel(a_ref, b_ref, gate_ref, h_ref, wa_ref, wb_ref, wo_ref, n2g_ref, ho_ref, h2_ref, *, T, P):
    t = pl.program_id(1)
    ya = _mm(a_ref[0], wa_ref[...])
    yb = _mm(b_ref[0], wb_ref[...])
    g0 = gate_ref[0, :, 0:D_MODEL].astype(f32)
    g1 = gate_ref[0, :, D_MODEL:2 * D_MODEL].astype(f32)
    mixed = g0 * ya + g1 * yb
    hnew = h_ref[0] + _mm(mixed.astype(bf16), wo_ref[...])
    hnew = jnp.where(_row_valid(t, T, P), hnew, 0.0)
    ho_ref[0] = hnew
    h2_ref[0] = _rms_rows(hnew, n2g_ref[...]).astype(bf16)


def _merge(attn, ob, gates, h, wa, wb, wo, n2g, *, T, P):
    B, Lp, D = h.shape
    tok = lambda w: pl.BlockSpec((1, T, w), lambda b, t: (b, t, 0))
    ins = [attn, ob, gates, h, wa, wb, wo, n2g]
    return pl.pallas_call(
        functools.partial(_merge_kernel, T=T, P=P),
        grid=(B, Lp // T),
        in_specs=[tok(FOX_W), tok(GDN_W), tok(2 * D_MODEL), tok(D)] + [_const_spec(a.shape) for a in ins[4:]],
        out_specs=[tok(D), tok(D)],
        out_shape=[jax.ShapeDtypeStruct((B, Lp, D), f32), jax.ShapeDtypeStruct((B, Lp, D), bf16)],
        input_output_aliases={3: 0},
        compiler_params=pltpu.CompilerParams(
            dimension_semantics=("parallel", "parallel"), vmem_limit_bytes=VMEM_LIMIT),
        name="merge",
    )(*ins)


def _ffn_kernel(h2_ref, h_ref, wup_ref, cw_ref, wdn_ref, ho_ref, halo_ref, gbuf, vbuf, *, T, FC):
    t = pl.program_id(1)

    @pl.when(t == 0)
    def _():
        halo_ref[...] = jnp.zeros_like(halo_ref)

    x = h2_ref[0]
    acc = h_ref[0]

    def conv(buf, cs):
        y = cw_ref[FFN_CONV - 1:FFN_CONV, cs] * buf[SUBLANES:SUBLANES + T, :]
        for j in range(1, FFN_CONV):
            y = y + cw_ref[FFN_CONV - 1 - j:FFN_CONV - j, cs] * buf[SUBLANES - j:SUBLANES - j + T, :]
        return y

    for c in range(D_FF // FC):
        gs = slice(c * FC, (c + 1) * FC)
        vs = slice(D_FF + c * FC, D_FF + (c + 1) * FC)
        gbuf[0:SUBLANES, :] = halo_ref[:, gs]
        vbuf[0:SUBLANES, :] = halo_ref[:, vs]
        gbuf[SUBLANES:SUBLANES + T, :] = _mm(x, wup_ref[:, gs])
        vbuf[SUBLANES:SUBLANES + T, :] = _mm(x, wup_ref[:, vs])
        halo_ref[:, gs] = gbuf[T:T + SUBLANES, :]
        halo_ref[:, vs] = vbuf[T:T + SUBLANES, :]
        act = _silu(conv(gbuf, gs)) * conv(vbuf, vs)
        acc = acc + _mm(act.astype(bf16), wdn_ref[gs, :])
    ho_ref[0] = acc


def _ffn(h2, h, wup, cw, wdn, *, T):
    B, Lp, D = h.shape
    FC = FFN_COL_CHUNK
    tok = lambda w: pl.BlockSpec((1, T, w), lambda b, t: (b, t, 0))
    ins = [h2, h, wup, cw, wdn]
    return pl.pallas_call(
        functools.partial(_ffn_kernel, T=T, FC=FC),
        grid=(B, Lp // T),
        in_specs=[tok(D), tok(D)] + [_const_spec(a.shape) for a in ins[2:]],
        out_specs=tok(D),
        out_shape=jax.ShapeDtypeStruct((B, Lp, D), f32),
        scratch_shapes=[pltpu.VMEM((SUBLANES, 2 * D_FF), f32),
                        pltpu.VMEM((SUBLANES + T, FC), f32),
                        pltpu.VMEM((SUBLANES + T, FC), f32)],
        input_output_aliases={1: 0},
        compiler_params=pltpu.CompilerParams(
            dimension_semantics=("parallel", "arbitrary"), vmem_limit_bytes=VMEM_LIMIT),
        name="ffn",
    )(*ins)


def _prep_layer_params(norm1_g, w_in, fox_f_bias, fox_q_norm_g, fox_k_norm_g, gdn_conv_w, gdn_a_log,
                       gdn_dt_bias, gdn_norm_g, w_branch_a, w_branch_b, w_out, norm2_g, w_up,
                       ffn_conv_w, w_down):
    nl = w_in.shape[0]
    wsm = jnp.zeros((nl, D_MODEL, LANES), f32)
    wf = w_in[:, :, _OFF_FLOGIT:_OFF_FLOGIT + FOX_HEADS]
    for j in range(3):
        wsm = wsm.at[:, :, _SM_F + 8 * j:_SM_F + 8 * (j + 1)].set(wf)
    wsm = wsm.at[:, :, _SM_B:_SM_B + GDN_HEADS].set(w_in[:, :, _OFF_BLOGIT:_OFF_BLOGIT + GDN_HEADS])
    wsm = wsm.at[:, :, _SM_A:_SM_A + GDN_HEADS].set(w_in[:, :, _OFF_ALOGIT:_OFF_ALOGIT + GDN_HEADS])
    smb = jnp.zeros((nl, 1, LANES), f32)
    for j in range(3):
        smb = smb.at[:, 0, _SM_F + 8 * j:_SM_F + 8 * (j + 1)].set(fox_f_bias.astype(f32))
    smb = smb.at[:, 0, _SM_A:_SM_A + GDN_HEADS].set(gdn_dt_bias.astype(f32))
    alog = jnp.zeros((nl, 1, LANES), f32).at[:, 0, _SM_A:_SM_A + GDN_HEADS].set(gdn_a_log.astype(f32))
    row = lambda a: a.astype(f32)[:, None, :]
    return dict(
        n1g=row(norm1_g),
        wfox=w_in[:, :, _OFF_FQ:_OFF_FQ + 3 * FOX_W].astype(bf16),
        wsm=wsm.astype(bf16), smb=smb, alog=alog,
        qg=row(jnp.tile(fox_q_norm_g, (1, FOX_HEADS))),
        kg=row(jnp.tile(fox_k_norm_g, (1, FOX_HEADS))),
        wgdn=w_in[:, :, _OFF_GQ:_OFF_GQ + 3 * GDN_W].astype(bf16),
        wgz=w_in[:, :, _OFF_GZ:_OFF_GZ + GDN_W].astype(bf16),
        wgate=w_in[:, :, _OFF_GATE:_OFF_GATE + 2 * D_MODEL].astype(bf16),
        gcw=gdn_conv_w.astype(f32),
        gng=row(jnp.tile(gdn_norm_g, (1, GDN_HEADS))),
        wa=w_branch_a.astype(bf16), wb=w_branch_b.astype(bf16), wo=w_out.astype(bf16),
        n2g=row(norm2_g),
        wup=w_up.astype(bf16), fcw=ffn_conv_w.astype(f32), wdn=w_down.astype(bf16),
    )


def _forward(x, meta_tokens, params, *, T):
    B, S, D = x.shape
    L = N_META + S
    Lp = -(-L // T) * T
    P = Lp - L
    meta = jnp.broadcast_to(meta_tokens.astype(x.dtype)[None], (B, N_META, D))
    h = jnp.concatenate([jnp.zeros((B, P, D), x.dtype), meta, x], axis=1)
    p = _prep_layer_params(*params)
    fox_consts = _fox_consts(T)
    gdn_consts = _gdn_consts(T, GDN_CHUNK)
    for l in range(p["wfox"].shape[0]):
        w = {k: v[l] for k, v in p.items()}
        qa, ka, va, sm = _fox_proj(h, w["n1g"], w["wfox"], w["wsm"], w["smb"], w["alog"], w["qg"], w["kg"],
                                   fox_consts, T=T, P=P)
        qkv, gz, gates = _gdn_proj(h, w["n1g"], w["wgdn"], w["wgz"], w["wgate"], T=T)
        attn = _fox_attn(qa, ka, va, TQ=T)
        ob = _gdn(qkv, sm, gz, w["gcw"], w["gng"], gdn_consts, T=T, C=GDN_CHUNK)
        h, h2 = _merge(attn, ob, gates, h, w["wa"], w["wb"], w["wo"], w["n2g"], T=T, P=P)
        h = _ffn(h2, h, w["wup"], w["fcw"], w["wdn"], T=T)
    return h[:, P + N_META:]


def kernel(x, meta_tokens, norm1_g, w_in, fox_f_bias, fox_q_norm_g, fox_k_norm_g, gdn_conv_w, gdn_a_log,
           gdn_dt_bias, gdn_norm_g, w_branch_a, w_branch_b, w_out, norm2_g, w_up, ffn_conv_w, w_down):
    params = (norm1_g, w_in, fox_f_bias, fox_q_norm_g, fox_k_norm_g, gdn_conv_w, gdn_a_log, gdn_dt_bias,
              gdn_norm_g, w_branch_a, w_branch_b, w_out, norm2_g, w_up, ffn_conv_w, w_down)
    return _forward(x, meta_tokens, params, T=TOKEN_TILE)
```

```python
import functools

import numpy as np
import jax
import jax.numpy as jnp
from jax import lax
from jax.experimental import pallas as pl
from jax.experimental.pallas import tpu as pltpu

D_MODEL = 1024
DEPTH = 4
N_META = 16
EPS = 1e-6
NEG_INF = -1e30
FOX_HEADS = 8
FOX_HEAD_DIM = 64
FOX_W = FOX_HEADS * FOX_HEAD_DIM
GDN_HEADS = 8
GDN_HEAD_DIM = 128
GDN_CHUNK = 64
GDN_CONV = 4
GDN_W = GDN_HEADS * GDN_HEAD_DIM
D_FF = 2816
FFN_CONV = 3
FFN_COL_CHUNK = 1408

LANES = 128
SUBLANES = 8
MXU_DIM = 256
TOKEN_TILE = 3 * MXU_DIM
VMEM_LIMIT = 56 * 1024 * 1024
LOG2E = float(np.log2(np.e))
FOX_VT_ROWS = 80


def _tile_plan(align):
    half = align // 2
    return dict(fox_proj=align, gdn_proj=half, attn=align, key_block=align // 3 if align % 3 == 0 else half,
                gdn=half, merge=align, ffn=half)

_OFF_FQ = 0
_OFF_FLOGIT = 3 * FOX_W
_OFF_GQ = _OFF_FLOGIT + FOX_HEADS
_OFF_BLOGIT = _OFF_GQ + 3 * GDN_W
_OFF_ALOGIT = _OFF_BLOGIT + GDN_HEADS
_OFF_GZ = _OFF_ALOGIT + GDN_HEADS
_OFF_GATE = _OFF_GZ + GDN_W

_SM_F = 0
_SM_ONE = 24
_SM_B = 32
_SM_A = 40
_SM_MASK = 32

_AUG = FOX_HEAD_DIM

bf16 = jnp.bfloat16
f32 = jnp.float32


def _mm(a, b):
    return jnp.dot(a, b, preferred_element_type=f32)


def _mm_nt(a, b):
    return lax.dot_general(a, b, (((1,), (1,)), ((), ())), preferred_element_type=f32)


def _mm_tn(a, b):
    return lax.dot_general(a, b, (((0,), (0,)), ((), ())), preferred_element_type=f32)


def _split3(x):
    p1 = x.astype(bf16)
    r1 = x - p1.astype(f32)
    p2 = r1.astype(bf16)
    p3 = (r1 - p2.astype(f32)).astype(bf16)
    return p1, p2, p3


def _mm3(m01, x):
    p1, p2, p3 = _split3(x)
    return (_mm(m01, p3) + _mm(m01, p2)) + _mm(m01, p1)


def _sigmoid(x):
    return 1.0 / (1.0 + jnp.exp(-x))


def _silu(x):
    return x * _sigmoid(x)


def _softplus(x):
    return jnp.maximum(x, 0.0) + jnp.log1p(jnp.exp(-jnp.abs(x)))


def _rms_rows(x, g):
    ms = jnp.mean(x * x, axis=-1, keepdims=True)
    return x * lax.rsqrt(ms + EPS) * g


def _row_valid(t, T, P):
    pos = t * T + lax.broadcasted_iota(jnp.int32, (T, 1), 0)
    return pos >= P


def _fox_proj_kernel(h_ref, n1g_ref, wqk_ref, wvt_ref, wsm_ref, smb_ref, alog_ref, qg_ref, kg_ref,
                     tri_ref, blk_ref, eq_ref, ek_ref,
                     qa_ref, ka_ref, vt_ref, sm_ref, carry_ref, *, T, P):
    t = pl.program_id(1)

    @pl.when(t == 0)
    def _():
        carry_ref[...] = jnp.zeros_like(carry_ref)

    hn = _rms_rows(h_ref[0], n1g_ref[...]).astype(bf16)
    valid = _row_valid(t, T, P)
    lane = lax.broadcasted_iota(jnp.int32, (T, LANES), 1)

    ysm = _mm(hn, wsm_ref[...]) + smb_ref[...]
    logf = -_softplus(-ysm)
    logf = jnp.where(lane < _SM_ONE, jnp.where(valid, logf, 0.0), 0.0)
    F = _mm3(tri_ref[...], logf) + carry_ref[...]
    carry_ref[...] = F[T - 1:T, :]
    beta = _sigmoid(ysm)
    g = -jnp.exp(alog_ref[...]) * _softplus(ysm)
    sm_ref[0] = jnp.where(lane < _SM_B, 0.0,
                          jnp.where(lane < _SM_A, beta, jnp.where(lane < _SM_A + GDN_HEADS, g, 0.0)))

    F2 = F * LOG2E
    f1 = F2.astype(bf16).astype(f32)
    f2 = (F2 - f1).astype(bf16).astype(f32)
    f3 = (F2 - f1 - f2).astype(bf16).astype(f32)
    fsel = jnp.where(lane < 8, f1, jnp.where(lane < 16, f2, f3))
    key_bias = jnp.where(valid, 0.0, NEG_INF)
    pq = jnp.where(lane < _SM_ONE, fsel, jnp.where(lane < _SM_ONE + 8, 1.0, 0.0)).astype(bf16)
    pk = jnp.where(lane < _SM_ONE, -fsel,
                   jnp.where(lane < _SM_ONE + 8, 1.0,
                             jnp.where(lane < _SM_MASK + 8, key_bias, 0.0))).astype(bf16)

    yf = _mm(hn, wqk_ref[...])
    q = yf[:, :FOX_W]
    k = yf[:, FOX_W:]
    inv_d = 1.0 / FOX_HEAD_DIM
    qss = _mm((q * q).astype(bf16), blk_ref[...])
    kss = _mm((k * k).astype(bf16), blk_ref[...])
    qn = q * lax.rsqrt(qss * inv_d + EPS) * qg_ref[...] * (FOX_HEAD_DIM ** -0.5 * LOG2E)
    kn = k * lax.rsqrt(kss * inv_d + EPS) * kg_ref[...]

    q_aug = _mm(jnp.concatenate([qn.astype(bf16), pq], axis=1), eq_ref[...])
    k_aug = _mm(jnp.concatenate([kn.astype(bf16), pk], axis=1), ek_ref[...])
    vt = _mm_nt(wvt_ref[...], hn)
    tail = jnp.where(lax.broadcasted_iota(jnp.int32, (FOX_VT_ROWS - FOX_HEAD_DIM, T), 0) == 0, 1.0, 0.0)
    for h in range(FOX_HEADS):
        sl = slice(h * LANES, (h + 1) * LANES)
        qa_ref[0, h] = q_aug[:, sl].astype(bf16)
        ka_ref[0, h] = k_aug[:, sl].astype(bf16)
        vt_ref[0, h, 0:FOX_HEAD_DIM, :] = vt[h * FOX_HEAD_DIM:(h + 1) * FOX_HEAD_DIM, :].astype(bf16)
        vt_ref[0, h, FOX_HEAD_DIM:FOX_VT_ROWS, :] = tail.astype(bf16)


def _fox_consts(T):
    tri = np.tril(np.ones((T, T), np.float32))
    blk = np.kron(np.eye(FOX_HEADS, dtype=np.float32), np.ones((FOX_HEAD_DIM, FOX_HEAD_DIM), np.float32))
    eq = np.zeros((FOX_W + LANES, FOX_HEADS * LANES), np.float32)
    ek = np.zeros_like(eq)
    for h in range(FOX_HEADS):
        base = h * LANES
        for d in range(FOX_HEAD_DIM):
            eq[h * FOX_HEAD_DIM + d, base + d] = 1.0
            ek[h * FOX_HEAD_DIM + d, base + d] = 1.0
        for j in range(3):
            eq[FOX_W + _SM_F + 8 * j + h, base + _AUG + j] = 1.0
            eq[FOX_W + _SM_ONE + h, base + _AUG + 3 + j] = 1.0
            ek[FOX_W + _SM_ONE + h, base + _AUG + j] = 1.0
            ek[FOX_W + _SM_F + 8 * j + h, base + _AUG + 3 + j] = 1.0
        eq[FOX_W + _SM_ONE + h, base + _AUG + 6] = 1.0
        ek[FOX_W + _SM_MASK + h, base + _AUG + 6] = 1.0
    as_bf = lambda a: jnp.asarray(a, bf16)
    return as_bf(tri), as_bf(blk), as_bf(eq), as_bf(ek)


def _const_spec(shape):
    nd = len(shape)
    return pl.BlockSpec(shape, lambda *_: (0,) * nd, pipeline_mode=pl.Buffered(1))


def _fox_proj(h, n1g, wqk, wvt, wsm, smb, alog, qg, kg, consts, *, T, P):
    B, Lp, D = h.shape
    nT = Lp // T
    tri, blk, eq, ek = consts
    head_spec = pl.BlockSpec((1, FOX_HEADS, T, LANES), lambda b, t: (b, 0, t, 0))
    head_shape = jax.ShapeDtypeStruct((B, FOX_HEADS, Lp, LANES), bf16)
    ins = [h, n1g, wqk, wvt, wsm, smb, alog, qg, kg, tri, blk, eq, ek]
    in_specs = [pl.BlockSpec((1, T, D), lambda b, t: (b, t, 0))] + [_const_spec(a.shape) for a in ins[1:]]
    return pl.pallas_call(
        functools.partial(_fox_proj_kernel, T=T, P=P),
        grid=(B, nT),
        in_specs=in_specs,
        out_specs=[head_spec, head_spec,
                   pl.BlockSpec((1, FOX_HEADS, FOX_VT_ROWS, T), lambda b, t: (b, 0, 0, t)),
                   pl.BlockSpec((1, T, LANES), lambda b, t: (b, t, 0))],
        out_shape=[head_shape, head_shape,
                   jax.ShapeDtypeStruct((B, FOX_HEADS, FOX_VT_ROWS, Lp), bf16),
                   jax.ShapeDtypeStruct((B, Lp, LANES), f32)],
        scratch_shapes=[pltpu.VMEM((1, LANES), f32)],
        compiler_params=pltpu.CompilerParams(
            dimension_semantics=("parallel", "arbitrary"), vmem_limit_bytes=VMEM_LIMIT),
        name="fox_proj",
    )(*ins)


def _gdn_proj_kernel(h_ref, n1g_ref, wgdn_ref, wgz_ref, wgate_ref, qkv_ref, gz_ref, gate_ref):
    hn = _rms_rows(h_ref[0], n1g_ref[...]).astype(bf16)
    for c in range(3):
        sl = slice(c * GDN_W, (c + 1) * GDN_W)
        qkv_ref[0, :, sl] = _mm(hn, wgdn_ref[:, sl])
    gz_ref[0] = _silu(_mm(hn, wgz_ref[...])).astype(bf16)
    for c in range(2):
        sl = slice(c * D_MODEL, (c + 1) * D_MODEL)
        gate_ref[0, :, sl] = _sigmoid(_mm(hn, wgate_ref[:, sl])).astype(bf16)


def _gdn_proj(h, n1g, wgdn, wgz, wgate, *, T):
    B, Lp, D = h.shape
    tok = lambda w: pl.BlockSpec((1, T, w), lambda b, t: (b, t, 0))
    ins = [h, n1g, wgdn, wgz, wgate]
    return pl.pallas_call(
        _gdn_proj_kernel,
        grid=(B, Lp // T),
        in_specs=[tok(D)] + [_const_spec(a.shape) for a in ins[1:]],
        out_specs=[tok(3 * GDN_W), tok(GDN_W), tok(2 * D_MODEL)],
        out_shape=[jax.ShapeDtypeStruct((B, Lp, 3 * GDN_W), f32),
                   jax.ShapeDtypeStruct((B, Lp, GDN_W), bf16),
                   jax.ShapeDtypeStruct((B, Lp, 2 * D_MODEL), bf16)],
        compiler_params=pltpu.CompilerParams(
            dimension_semantics=("parallel", "parallel"), vmem_limit_bytes=VMEM_LIMIT),
        name="gdn_proj",
    )(*ins)


def _fox_attn_kernel(q_ref, k_ref, vt_ref, o_ref, m_s, acc_s, *, TQ, KB):
    qi = pl.program_id(2)
    pair = range(2)
    m_s[...] = jnp.full_like(m_s, NEG_INF)
    acc_s[...] = jnp.zeros_like(acc_s)
    qs = [q_ref[0, hh] for hh in pair]

    def scores(koff, c0):
        return [_mm_nt(k_ref[0, hh, pl.ds(koff, KB), :], qs[hh][c0:, :]) for hh in pair]

    def absorb(sT, koff, c0, masked):
        w = TQ - c0
        if masked:
            keep = (lax.broadcasted_iota(jnp.int32, (KB, w), 0) <= lax.broadcasted_iota(jnp.int32, (KB, w), 1))
            sT = [jnp.where(keep, s, NEG_INF) for s in sT]
        m_old = [m_s[hh, :, c0:] for hh in pair]
        m_new = [jnp.maximum(mo, jnp.max(s, axis=0, keepdims=True)) for mo, s in zip(m_old, sT)]
        pT = [jnp.exp2(s - mn).astype(bf16) for s, mn in zip(sT, m_new)]
        pv = [_mm(vt_ref[0, hh, :, pl.ds(koff, KB)], p) for hh, p in zip(pair, pT)]
        for hh in pair:
            acc_s[hh, :, c0:] = jnp.exp2(m_old[hh] - m_new[hh]) * acc_s[hh, :, c0:] + pv[hh]
            m_s[hh, :, c0:] = m_new[hh]

    nsub = TQ // KB

    def below_diagonal(sb, carry):
        offs = [pl.multiple_of(sb * TQ + j * KB, KB) for j in range(nsub)]
        sTs = [scores(o, 0) for o in offs]
        for o, sT in zip(offs, sTs):
            absorb(sT, o, 0, False)
        return carry

    lax.fori_loop(0, qi, below_diagonal, 0)
    offs = [pl.multiple_of(qi * TQ + j * KB, KB) for j in range(nsub)]
    sTs = [scores(o, j * KB) for j, o in enumerate(offs)]
    for j, (o, sT) in enumerate(zip(offs, sTs)):
        absorb(sT, o, j * KB, True)

    outs = []
    for hh in pair:
        acc = acc_s[hh]
        outs.append(acc[0:FOX_HEAD_DIM, :] * (1.0 / acc[FOX_HEAD_DIM:FOX_HEAD_DIM + 1, :]))
    o_ref[0] = jnp.transpose(jnp.concatenate(outs, axis=0)).astype(bf16)


def _fox_attn(qa, ka, vt, *, TQ, KB):
    B, H, Lp, _ = qa.shape
    return pl.pallas_call(
        functools.partial(_fox_attn_kernel, TQ=TQ, KB=KB),
        grid=(B, H // 2, Lp // TQ),
        in_specs=[pl.BlockSpec((1, 2, TQ, LANES), lambda b, hp, qi: (b, hp, qi, 0)),
                  pl.BlockSpec((1, 2, Lp, LANES), lambda b, hp, qi: (b, hp, 0, 0)),
                  pl.BlockSpec((1, 2, FOX_VT_ROWS, Lp), lambda b, hp, qi: (b, hp, 0, 0))],
        out_specs=pl.BlockSpec((1, TQ, LANES), lambda b, hp, qi: (b, qi, hp)),
        out_shape=jax.ShapeDtypeStruct((B, Lp, FOX_W), bf16),
        scratch_shapes=[pltpu.VMEM((2, 1, TQ), f32),
                        pltpu.VMEM((2, FOX_VT_ROWS, TQ), f32)],
        compiler_params=pltpu.CompilerParams(
            dimension_semantics=("parallel", "parallel", "arbitrary"), vmem_limit_bytes=VMEM_LIMIT),
        name="fox_attn",
    )(qa, ka, vt)


def _gdn_kernel(x_ref, sm_ref, gz_ref, cw_ref, ng_ref, tribd_ref, onebd_ref, o_ref,
                S_ref, halo_ref, xbuf, G_s, qg_s, qn_s, kn_s, kb_s, kd_s, vb_s, uh_s, w_s, aqk_s, *, T, C):
    t = pl.program_id(1)
    H, Dh = GDN_HEADS, GDN_HEAD_DIM

    @pl.when(t == 0)
    def _():
        S_ref[...] = jnp.zeros_like(S_ref)
        halo_ref[...] = jnp.zeros_like(halo_ref)

    sm = sm_ref[0]
    lane = lax.broadcasted_iota(jnp.int32, (T, LANES), 1)
    gl = jnp.where(lane < _SM_A, 0.0, jnp.where(lane < _SM_A + H, sm, 0.0))
    G = _mm3(tribd_ref[...], gl)
    Glast = _mm3(onebd_ref[...], gl)
    G_s[...] = G
    eG = jnp.exp(G)
    eKd = jnp.exp(Glast - G)

    def conv_silu(cb):
        cs = slice(cb * Dh, (cb + 1) * Dh)
        xbuf[0:SUBLANES, :] = halo_ref[:, cs]
        xbuf[SUBLANES:SUBLANES + T, :] = x_ref[0, :, cs]
        y = cw_ref[GDN_CONV - 1:GDN_CONV, cs] * xbuf[SUBLANES:SUBLANES + T, :]
        for j in range(1, GDN_CONV):
            y = y + cw_ref[GDN_CONV - 1 - j:GDN_CONV - j, cs] * xbuf[SUBLANES - j:SUBLANES - j + T, :]
        return _silu(y)

    def l2n(a):
        return a * lax.rsqrt(jnp.sum(a * a, axis=-1, keepdims=True) + EPS)

    for h in range(H):
        qh = l2n(conv_silu(h)) * (Dh ** -0.5)
        kh = l2n(conv_silu(H + h))
        vh = conv_silu(2 * H + h)
        beta = sm[:, _SM_B + h:_SM_B + h + 1]
        eg = eG[:, _SM_A + h:_SM_A + h + 1]
        ekd = eKd[:, _SM_A + h:_SM_A + h + 1]
        kbh = kh * beta
        qn_s[h] = qh.astype(bf16)
        qg_s[h] = (qh * eg).astype(bf16)
        kn_s[h] = kh.astype(bf16)
        kb_s[h] = kbh.astype(bf16)
        kd_s[h] = (kh * ekd).astype(bf16)
        vb_s[h, :, 0:Dh] = (vh * beta).astype(bf16)
        vb_s[h, :, Dh:2 * Dh] = (kbh * eg).astype(bf16)
    halo_ref[...] = x_ref[0, T - SUBLANES:T, :]

    ri = lax.broadcasted_iota(jnp.int32, (C, C), 0)
    ci = lax.broadcasted_iota(jnp.int32, (C, C), 1)
    strict = ri > ci
    incl = ri >= ci
    eye = (ri == ci).astype(f32)
    n_double = int(np.log2(C)) - 1

    GROUP = 2
    heads = range(H)

    def phase_a(ci, carry):
        r0 = pl.multiple_of(ci * (GROUP * C), GROUP * C)
        Gblk = G_s[pl.ds(r0, GROUP * C), :]
        GblkT = jnp.transpose(Gblk)
        probs = [(cc, h) for cc in range(GROUP) for h in heads]
        rows = [pl.ds(r0 + cc * C, C) for cc, _ in probs]
        kn = [kn_s[h, r, :] for (_, h), r in zip(probs, rows)]
        kk = [_mm_nt(kb_s[h, r, :], k) for (_, h), r, k in zip(probs, rows, kn)]
        qk = [_mm_nt(qn_s[h, r, :], k) for (_, h), r, k in zip(probs, rows, kn)]
        dec = []
        for cc, h in probs:
            gcol = Gblk[cc * C:(cc + 1) * C, _SM_A + h:_SM_A + h + 1]
            grow = GblkT[_SM_A + h:_SM_A + h + 1, cc * C:(cc + 1) * C]
            dec.append(jnp.exp(jnp.where(incl, gcol - grow, NEG_INF)))
        for (_, h), r, a, d in zip(probs, rows, qk, dec):
            aqk_s[h, r, :] = jnp.where(incl, a * d, 0.0).astype(bf16)
        Nk = [jnp.where(strict, -(a * d), 0.0) for a, d in zip(kk, dec)]
        Tm = [eye + n for n in Nk]
        for _ in range(n_double):
            Nb = [n.astype(bf16) for n in Nk]
            Nk = [_mm(n, n) for n in Nb]
            Tm = [tm + _mm(tm.astype(bf16), n.astype(bf16)) for tm, n in zip(Tm, Nk)]
        for (_, h), r, tm in zip(probs, rows, Tm):
            uw = _mm(tm.astype(bf16), vb_s[h, r, :])
            uh_s[h, r, :] = uw[:, 0:Dh]
            w_s[h, r, :] = uw[:, Dh:2 * Dh].astype(bf16)
        return carry

    lax.fori_loop(0, T // (GROUP * C), phase_a, 0)

    def phase_b(c, carry):
        r0 = pl.multiple_of(c * C, C)
        rows = pl.ds(r0, C)
        Gc = G_s[rows, :]
        S = [S_ref[h] for h in heads]
        Sb = [s.astype(bf16) for s in S]
        wS = [_mm(w_s[h, rows, :], Sb[h]) for h in heads]
        qS = [_mm(qg_s[h, rows, :], Sb[h]) for h in heads]
        Ub = [(uh_s[h, rows, :] - wS[h]).astype(bf16) for h in heads]
        aU = [_mm(aqk_s[h, rows, :], Ub[h]) for h in heads]
        kU = [_mm_tn(kd_s[h, rows, :], Ub[h]) for h in heads]
        for h in heads:
            glast = Gc[C - 1:C, _SM_A + h:_SM_A + h + 1]
            S_ref[h] = S[h] * jnp.exp(glast) + kU[h]
        for h in heads:
            cs = slice(h * Dh, (h + 1) * Dh)
            on = _rms_rows(qS[h] + aU[h], ng_ref[:, cs])
            o_ref[0, rows, cs] = (on * gz_ref[0, rows, cs].astype(f32)).astype(bf16)
        return carry

    lax.fori_loop(0, T // C, phase_b, 0)


def _gdn_consts(T, C):
    n = T // C
    tribd = np.kron(np.eye(n, dtype=np.float32), np.tril(np.ones((C, C), np.float32)))
    onebd = np.kron(np.eye(n, dtype=np.float32), np.ones((C, C), np.float32))
    return jnp.asarray(tribd, bf16), jnp.asarray(onebd, bf16)


def _gdn(qkv, sm, gz, cw, ng, consts, *, T, C):
    B, Lp, _ = qkv.shape
    H, Dh = GDN_HEADS, GDN_HEAD_DIM
    tribd, onebd = consts
    tok = lambda w: pl.BlockSpec((1, T, w), lambda b, t: (b, t, 0))
    ins = [qkv, sm, gz, cw, ng, tribd, onebd]
    head_bf = pltpu.VMEM((H, T, Dh), bf16)
    return pl.pallas_call(
        functools.partial(_gdn_kernel, T=T, C=C),
        grid=(B, Lp // T),
        in_specs=[tok(3 * GDN_W), tok(LANES), tok(GDN_W)] + [_const_spec(a.shape) for a in ins[3:]],
        out_specs=tok(GDN_W),
        out_shape=jax.ShapeDtypeStruct((B, Lp, GDN_W), bf16),
        scratch_shapes=[pltpu.VMEM((H, Dh, Dh), f32),
                        pltpu.VMEM((SUBLANES, 3 * GDN_W), f32),
                        pltpu.VMEM((SUBLANES + T, Dh), f32),
                        pltpu.VMEM((T, LANES), f32),
                        head_bf, head_bf, head_bf, head_bf, head_bf,
                        pltpu.VMEM((H, T, 2 * Dh), bf16),
                        pltpu.VMEM((H, T, Dh), f32),
                        head_bf,
                        pltpu.VMEM((H, T, C), bf16)],
        compiler_params=pltpu.CompilerParams(
            dimension_semantics=("parallel", "arbitrary"), vmem_limit_bytes=VMEM_LIMIT),
        name="gdn",
    )(*ins)


def _merge_kernel(a_ref, b_ref, gate_ref, h_ref, wa_ref, wb_ref, wo_ref, n2g_ref, ho_ref, h2_ref, *, T, P):
    t = pl.program_id(1)
    ya = _mm(a_ref[0], wa_ref[...])
    yb = _mm(b_ref[0], wb_ref[...])
    g0 = gate_ref[0, :, 0:D_MODEL].astype(f32)
    g1 = gate_ref[0, :, D_MODEL:2 * D_MODEL].astype(f32)
    mixed = g0 * ya + g1 * yb
    hnew = h_ref[0] + _mm(mixed.astype(bf16), wo_ref[...])
    hnew = jnp.where(_row_valid(t, T, P), hnew, 0.0)
    ho_ref[0] = hnew
    h2_ref[0] = _rms_rows(hnew, n2g_ref[...]).astype(bf16)


def _merge(attn, ob, gates, h, wa, wb, wo, n2g, *, T, P):
    B, Lp, D = h.shape
    tok = lambda w: pl.BlockSpec((1, T, w), lambda b, t: (b, t, 0))
    ins = [attn, ob, gates, h, wa, wb, wo, n2g]
    return pl.pallas_call(
        functools.partial(_merge_kernel, T=T, P=P),
        grid=(B, Lp // T),
        in_specs=[tok(FOX_W), tok(GDN_W), tok(2 * D_MODEL), tok(D)] + [_const_spec(a.shape) for a in ins[4:]],
        out_specs=[tok(D), tok(D)],
        out_shape=[jax.ShapeDtypeStruct((B, Lp, D), f32), jax.ShapeDtypeStruct((B, Lp, D), bf16)],
        input_output_aliases={3: 0},
        compiler_params=pltpu.CompilerParams(
            dimension_semantics=("parallel", "parallel"), vmem_limit_bytes=VMEM_LIMIT),
        name="merge",
    )(*ins)


def _ffn_kernel(h2_ref, h_ref, wup_ref, cw_ref, wdn_ref, ho_ref, halo_ref, gbuf, vbuf, *, T, FC):
    t = pl.program_id(1)

    @pl.when(t == 0)
    def _():
        halo_ref[...] = jnp.zeros_like(halo_ref)

    x = h2_ref[0]
    acc = h_ref[0]

    def conv(buf, cs):
        y = cw_ref[FFN_CONV - 1:FFN_CONV, cs] * buf[SUBLANES:SUBLANES + T, :]
        for j in range(1, FFN_CONV):
            y = y + cw_ref[FFN_CONV - 1 - j:FFN_CONV - j, cs] * buf[SUBLANES - j:SUBLANES - j + T, :]
        return y

    for c in range(D_FF // FC):
        gs = slice(c * FC, (c + 1) * FC)
        vs = slice(D_FF + c * FC, D_FF + (c + 1) * FC)
        gbuf[0:SUBLANES, :] = halo_ref[:, gs]
        vbuf[0:SUBLANES, :] = halo_ref[:, vs]
        gbuf[SUBLANES:SUBLANES + T, :] = _mm(x, wup_ref[:, gs])
        vbuf[SUBLANES:SUBLANES + T, :] = _mm(x, wup_ref[:, vs])
        halo_ref[:, gs] = gbuf[T:T + SUBLANES, :]
        halo_ref[:, vs] = vbuf[T:T + SUBLANES, :]
        act = _silu(conv(gbuf, gs)) * conv(vbuf, vs)
        acc = acc + _mm(act.astype(bf16), wdn_ref[gs, :])
    ho_ref[0] = acc


def _ffn(h2, h, wup, cw, wdn, *, T):
    B, Lp, D = h.shape
    FC = FFN_COL_CHUNK
    tok = lambda w: pl.BlockSpec((1, T, w), lambda b, t: (b, t, 0))
    ins = [h2, h, wup, cw, wdn]
    return pl.pallas_call(
        functools.partial(_ffn_kernel, T=T, FC=FC),
        grid=(B, Lp // T),
        in_specs=[tok(D), tok(D)] + [_const_spec(a.shape) for a in ins[2:]],
        out_specs=tok(D),
        out_shape=jax.ShapeDtypeStruct((B, Lp, D), f32),
        scratch_shapes=[pltpu.VMEM((SUBLANES, 2 * D_FF), f32),
                        pltpu.VMEM((SUBLANES + T, FC), f32),
                        pltpu.VMEM((SUBLANES + T, FC), f32)],
        input_output_aliases={1: 0},
        compiler_params=pltpu.CompilerParams(
            dimension_semantics=("parallel", "arbitrary"), vmem_limit_bytes=VMEM_LIMIT),
        name="ffn",
    )(*ins)


def _prep_layer_params(norm1_g, w_in, fox_f_bias, fox_q_norm_g, fox_k_norm_g, gdn_conv_w, gdn_a_log,
                       gdn_dt_bias, gdn_norm_g, w_branch_a, w_branch_b, w_out, norm2_g, w_up,
                       ffn_conv_w, w_down):
    nl = w_in.shape[0]
    wsm = jnp.zeros((nl, D_MODEL, LANES), f32)
    wf = w_in[:, :, _OFF_FLOGIT:_OFF_FLOGIT + FOX_HEADS]
    for j in range(3):
        wsm = wsm.at[:, :, _SM_F + 8 * j:_SM_F + 8 * (j + 1)].set(wf)
    wsm = wsm.at[:, :, _SM_B:_SM_B + GDN_HEADS].set(w_in[:, :, _OFF_BLOGIT:_OFF_BLOGIT + GDN_HEADS])
    wsm = wsm.at[:, :, _SM_A:_SM_A + GDN_HEADS].set(w_in[:, :, _OFF_ALOGIT:_OFF_ALOGIT + GDN_HEADS])
    smb = jnp.zeros((nl, 1, LANES), f32)
    for j in range(3):
        smb = smb.at[:, 0, _SM_F + 8 * j:_SM_F + 8 * (j + 1)].set(fox_f_bias.astype(f32))
    smb = smb.at[:, 0, _SM_A:_SM_A + GDN_HEADS].set(gdn_dt_bias.astype(f32))
    alog = jnp.zeros((nl, 1, LANES), f32).at[:, 0, _SM_A:_SM_A + GDN_HEADS].set(gdn_a_log.astype(f32))
    row = lambda a: a.astype(f32)[:, None, :]
    return dict(
        n1g=row(norm1_g),
        wqk=w_in[:, :, _OFF_FQ:_OFF_FQ + 2 * FOX_W].astype(bf16),
        wvt=jnp.swapaxes(w_in[:, :, _OFF_FQ + 2 * FOX_W:_OFF_FQ + 3 * FOX_W], 1, 2).astype(bf16),
        wsm=wsm.astype(bf16), smb=smb, alog=alog,
        qg=row(jnp.tile(fox_q_norm_g, (1, FOX_HEADS))),
        kg=row(jnp.tile(fox_k_norm_g, (1, FOX_HEADS))),
        wgdn=w_in[:, :, _OFF_GQ:_OFF_GQ + 3 * GDN_W].astype(bf16),
        wgz=w_in[:, :, _OFF_GZ:_OFF_GZ + GDN_W].astype(bf16),
        wgate=w_in[:, :, _OFF_GATE:_OFF_GATE + 2 * D_MODEL].astype(bf16),
        gcw=gdn_conv_w.astype(f32),
        gng=row(jnp.tile(gdn_norm_g, (1, GDN_HEADS))),
        wa=w_branch_a.astype(bf16), wb=w_branch_b.astype(bf16), wo=w_out.astype(bf16),
        n2g=row(norm2_g),
        wup=w_up.astype(bf16), fcw=ffn_conv_w.astype(f32), wdn=w_down.astype(bf16),
    )


def _forward(x, meta_tokens, params, *, T):
    B, S, D = x.shape
    L = N_META + S
    Lp = -(-L // T) * T
    P = Lp - L
    meta = jnp.broadcast_to(meta_tokens.astype(x.dtype)[None], (B, N_META, D))
    h = jnp.concatenate([jnp.zeros((B, P, D), x.dtype), meta, x], axis=1)
    p = _prep_layer_params(*params)
    tp = _tile_plan(T)
    fox_consts = _fox_consts(tp["fox_proj"])
    gdn_consts = _gdn_consts(tp["gdn"], GDN_CHUNK)
    for l in range(p["wqk"].shape[0]):
        w = {k: v[l] for k, v in p.items()}
        qa, ka, vt, sm = _fox_proj(h, w["n1g"], w["wqk"], w["wvt"], w["wsm"], w["smb"], w["alog"], w["qg"],
                                   w["kg"], fox_consts, T=tp["fox_proj"], P=P)
        qkv, gz, gates = _gdn_proj(h, w["n1g"], w["wgdn"], w["wgz"], w["wgate"], T=tp["gdn_proj"])
        attn = _fox_attn(qa, ka, vt, TQ=tp["attn"], KB=tp["key_block"])
        ob = _gdn(qkv, sm, gz, w["gcw"], w["gng"], gdn_consts, T=tp["gdn"], C=GDN_CHUNK)
        h, h2 = _merge(attn, ob, gates, h, w["wa"], w["wb"], w["wo"], w["n2g"], T=tp["merge"], P=P)
        h = _ffn(h2, h, w["wup"], w["fcw"], w["wdn"], T=tp["ffn"])
    return h[:, P + N_META:]


def kernel(x, meta_tokens, norm1_g, w_in, fox_f_bias, fox_q_norm_g, fox_k_norm_g, gdn_conv_w, gdn_a_log,
           gdn_dt_bias, gdn_norm_g, w_branch_a, w_branch_b, w_out, norm2_g, w_up, ffn_conv_w, w_down):
    params = (norm1_g, w_in, fox_f_bias, fox_q_norm_g, fox_k_norm_g, gdn_conv_w, gdn_a_log, gdn_dt_bias,
              gdn_norm_g, w_branch_a, w_branch_b, w_out, norm2_g, w_up, ffn_conv_w, w_down)
    return _forward(x, meta_tokens, params, T=TOKEN_TILE)
```

```python
import functools

import numpy as np
import jax
import jax.numpy as jnp
from jax import lax
from jax.experimental import pallas as pl
from jax.experimental.pallas import tpu as pltpu

D_MODEL = 1024
DEPTH = 4
N_META = 16
EPS = 1e-6
NEG_INF = -1e30
FOX_HEADS = 8
FOX_HEAD_DIM = 64
FOX_W = FOX_HEADS * FOX_HEAD_DIM
GDN_HEADS = 8
GDN_HEAD_DIM = 128
GDN_CHUNK = 64
GDN_CONV = 4
GDN_W = GDN_HEADS * GDN_HEAD_DIM
D_FF = 2816
FFN_CONV = 3
FFN_COL_CHUNK = 1408

LANES = 128
SUBLANES = 8
MXU_DIM = 256
TOKEN_TILE = 3 * MXU_DIM
VMEM_LIMIT = 56 * 1024 * 1024
LOG2E = float(np.log2(np.e))
FOX_VT_ROWS = 80


def _tile_plan(align):
    half = align // 2
    return dict(fox_proj=align, gdn_proj=half, attn=align, key_block=align // 3 if align % 3 == 0 else half,
                gdn=half, merge=align, ffn=half)

_OFF_FQ = 0
_OFF_FLOGIT = 3 * FOX_W
_OFF_GQ = _OFF_FLOGIT + FOX_HEADS
_OFF_BLOGIT = _OFF_GQ + 3 * GDN_W
_OFF_ALOGIT = _OFF_BLOGIT + GDN_HEADS
_OFF_GZ = _OFF_ALOGIT + GDN_HEADS
_OFF_GATE = _OFF_GZ + GDN_W

_SM_F = 0
_SM_ONE = 24
_SM_B = 32
_SM_A = 40
_SM_MASK = 32

_AUG = FOX_HEAD_DIM

bf16 = jnp.bfloat16
f32 = jnp.float32


def _mm(a, b):
    return jnp.dot(a, b, preferred_element_type=f32)


def _mm_nt(a, b):
    return lax.dot_general(a, b, (((1,), (1,)), ((), ())), preferred_element_type=f32)


def _mm_tn(a, b):
    return lax.dot_general(a, b, (((0,), (0,)), ((), ())), preferred_element_type=f32)


def _split3(x):
    p1 = x.astype(bf16)
    r1 = x - p1.astype(f32)
    p2 = r1.astype(bf16)
    p3 = (r1 - p2.astype(f32)).astype(bf16)
    return p1, p2, p3


def _mm3(m01, x):
    p1, p2, p3 = _split3(x)
    return (_mm(m01, p3) + _mm(m01, p2)) + _mm(m01, p1)


def _sigmoid(x):
    return 1.0 / (1.0 + jnp.exp(-x))


def _silu(x):
    return x * _sigmoid(x)


def _softplus(x):
    return jnp.maximum(x, 0.0) + jnp.log1p(jnp.exp(-jnp.abs(x)))


def _rms_rows(x, g):
    ms = jnp.mean(x * x, axis=-1, keepdims=True)
    return x * lax.rsqrt(ms + EPS) * g


def _row_valid(t, T, P):
    pos = t * T + lax.broadcasted_iota(jnp.int32, (T, 1), 0)
    return pos >= P


def _fox_proj_kernel(h_ref, n1g_ref, wqk_ref, wvt_ref, wsm_ref, smb_ref, alog_ref, qg_ref, kg_ref,
                     tri_ref, blk_ref, eq_ref, ek_ref,
                     qa_ref, ka_ref, vt_ref, sm_ref, carry_ref, *, T, P):
    t = pl.program_id(1)

    @pl.when(t == 0)
    def _():
        carry_ref[...] = jnp.zeros_like(carry_ref)

    hn = _rms_rows(h_ref[0], n1g_ref[...]).astype(bf16)
    valid = _row_valid(t, T, P)
    lane = lax.broadcasted_iota(jnp.int32, (T, LANES), 1)

    ysm = _mm(hn, wsm_ref[...]) + smb_ref[...]
    logf = -_softplus(-ysm)
    logf = jnp.where(lane < _SM_ONE, jnp.where(valid, logf, 0.0), 0.0)
    F = _mm3(tri_ref[...], logf) + carry_ref[...]
    carry_ref[...] = F[T - 1:T, :]
    beta = _sigmoid(ysm)
    g = -jnp.exp(alog_ref[...]) * _softplus(ysm)
    sm_ref[0] = jnp.where(lane < _SM_B, 0.0,
                          jnp.where(lane < _SM_A, beta, jnp.where(lane < _SM_A + GDN_HEADS, g, 0.0)))

    F2 = F * LOG2E
    f1 = F2.astype(bf16).astype(f32)
    f2 = (F2 - f1).astype(bf16).astype(f32)
    f3 = (F2 - f1 - f2).astype(bf16).astype(f32)
    fsel = jnp.where(lane < 8, f1, jnp.where(lane < 16, f2, f3))
    key_bias = jnp.where(valid, 0.0, NEG_INF)
    pq = jnp.where(lane < _SM_ONE, fsel, jnp.where(lane < _SM_ONE + 8, 1.0, 0.0)).astype(bf16)
    pk = jnp.where(lane < _SM_ONE, -fsel,
                   jnp.where(lane < _SM_ONE + 8, 1.0,
                             jnp.where(lane < _SM_MASK + 8, key_bias, 0.0))).astype(bf16)

    yf = _mm(hn, wqk_ref[...])
    q = yf[:, :FOX_W]
    k = yf[:, FOX_W:]
    inv_d = 1.0 / FOX_HEAD_DIM
    qss = _mm((q * q).astype(bf16), blk_ref[...])
    kss = _mm((k * k).astype(bf16), blk_ref[...])
    qn = q * lax.rsqrt(qss * inv_d + EPS) * qg_ref[...] * (FOX_HEAD_DIM ** -0.5 * LOG2E)
    kn = k * lax.rsqrt(kss * inv_d + EPS) * kg_ref[...]

    q_aug = _mm(jnp.concatenate([qn.astype(bf16), pq], axis=1), eq_ref[...])
    k_aug = _mm(jnp.concatenate([kn.astype(bf16), pk], axis=1), ek_ref[...])
    vt = _mm_nt(wvt_ref[...], hn)
    tail = jnp.where(lax.broadcasted_iota(jnp.int32, (FOX_VT_ROWS - FOX_HEAD_DIM, T), 0) == 0, 1.0, 0.0)
    for h in range(FOX_HEADS):
        sl = slice(h * LANES, (h + 1) * LANES)
        qa_ref[0, h] = q_aug[:, sl].astype(bf16)
        ka_ref[0, h] = k_aug[:, sl].astype(bf16)
        vt_ref[0, h, 0:FOX_HEAD_DIM, :] = vt[h * FOX_HEAD_DIM:(h + 1) * FOX_HEAD_DIM, :].astype(bf16)
        vt_ref[0, h, FOX_HEAD_DIM:FOX_VT_ROWS, :] = tail.astype(bf16)


def _fox_consts(T):
    tri = np.tril(np.ones((T, T), np.float32))
    blk = np.kron(np.eye(FOX_HEADS, dtype=np.float32), np.ones((FOX_HEAD_DIM, FOX_HEAD_DIM), np.float32))
    eq = np.zeros((FOX_W + LANES, FOX_HEADS * LANES), np.float32)
    ek = np.zeros_like(eq)
    for h in range(FOX_HEADS):
        base = h * LANES
        for d in range(FOX_HEAD_DIM):
            eq[h * FOX_HEAD_DIM + d, base + d] = 1.0
            ek[h * FOX_HEAD_DIM + d, base + d] = 1.0
        for j in range(3):
            eq[FOX_W + _SM_F + 8 * j + h, base + _AUG + j] = 1.0
            eq[FOX_W + _SM_ONE + h, base + _AUG + 3 + j] = 1.0
            ek[FOX_W + _SM_ONE + h, base + _AUG + j] = 1.0
            ek[FOX_W + _SM_F + 8 * j + h, base + _AUG + 3 + j] = 1.0
        eq[FOX_W + _SM_ONE + h, base + _AUG + 6] = 1.0
        ek[FOX_W + _SM_MASK + h, base + _AUG + 6] = 1.0
    as_bf = lambda a: jnp.asarray(a, bf16)
    return as_bf(tri), as_bf(blk), as_bf(eq), as_bf(ek)


def _const_spec(shape):
    nd = len(shape)
    return pl.BlockSpec(shape, lambda *_: (0,) * nd, pipeline_mode=pl.Buffered(1))


def _fox_proj(h, n1g, wqk, wvt, wsm, smb, alog, qg, kg, consts, *, T, P):
    B, Lp, D = h.shape
    nT = Lp // T
    tri, blk, eq, ek = consts
    head_spec = pl.BlockSpec((1, FOX_HEADS, T, LANES), lambda b, t: (b, 0, t, 0))
    head_shape = jax.ShapeDtypeStruct((B, FOX_HEADS, Lp, LANES), bf16)
    ins = [h, n1g, wqk, wvt, wsm, smb, alog, qg, kg, tri, blk, eq, ek]
    in_specs = [pl.BlockSpec((1, T, D), lambda b, t: (b, t, 0))] + [_const_spec(a.shape) for a in ins[1:]]
    return pl.pallas_call(
        functools.partial(_fox_proj_kernel, T=T, P=P),
        grid=(B, nT),
        in_specs=in_specs,
        out_specs=[head_spec, head_spec,
                   pl.BlockSpec((1, FOX_HEADS, FOX_VT_ROWS, T), lambda b, t: (b, 0, 0, t)),
                   pl.BlockSpec((1, T, LANES), lambda b, t: (b, t, 0))],
        out_shape=[head_shape, head_shape,
                   jax.ShapeDtypeStruct((B, FOX_HEADS, FOX_VT_ROWS, Lp), bf16),
                   jax.ShapeDtypeStruct((B, Lp, LANES), f32)],
        scratch_shapes=[pltpu.VMEM((1, LANES), f32)],
        compiler_params=pltpu.CompilerParams(
            dimension_semantics=("parallel", "arbitrary"), vmem_limit_bytes=VMEM_LIMIT),
        name="fox_proj",
    )(*ins)


def _gdn_proj_kernel(h_ref, n1g_ref, wgdn_ref, wgz_ref, wgate_ref, cw_ref, qkv_ref, gz_ref, gate_ref,
                     halo_ref, xbuf, *, T):
    t = pl.program_id(1)
    Dh = GDN_HEAD_DIM

    @pl.when(t == 0)
    def _():
        halo_ref[...] = jnp.zeros_like(halo_ref)

    hn = _rms_rows(h_ref[0], n1g_ref[...]).astype(bf16)
    for c in range(3):
        sl = slice(c * GDN_W, (c + 1) * GDN_W)
        xbuf[c, 0:SUBLANES, :] = halo_ref[:, sl]
        xbuf[c, SUBLANES:SUBLANES + T, :] = _mm(hn, wgdn_ref[:, sl])
        halo_ref[:, sl] = xbuf[c, T:T + SUBLANES, :]

    def other_projection(c):
        if c == 0:
            gz_ref[0] = _silu(_mm(hn, wgz_ref[...])).astype(bf16)
        else:
            sl = slice((c - 1) * D_MODEL, c * D_MODEL)
            gate_ref[0, :, sl] = _sigmoid(_mm(hn, wgate_ref[:, sl])).astype(bf16)

    for c in range(3):
        sl = slice(c * GDN_W, (c + 1) * GDN_W)
        other_projection(c)
        y = cw_ref[GDN_CONV - 1:GDN_CONV, sl] * xbuf[c, SUBLANES:SUBLANES + T, :]
        for j in range(1, GDN_CONV):
            y = y + cw_ref[GDN_CONV - 1 - j:GDN_CONV - j, sl] * xbuf[c, SUBLANES - j:SUBLANES - j + T, :]
        y = _silu(y)
        if c == 2:
            qkv_ref[0, :, sl] = y
        else:
            for h in range(GDN_HEADS):
                a = y[:, h * Dh:(h + 1) * Dh]
                a = a * lax.rsqrt(jnp.sum(a * a, axis=-1, keepdims=True) + EPS)
                if c == 0:
                    a = a * (Dh ** -0.5)
                qkv_ref[0, :, c * GDN_W + h * Dh:c * GDN_W + (h + 1) * Dh] = a


def _gdn_proj(h, n1g, wgdn, wgz, wgate, cw, *, T):
    B, Lp, D = h.shape
    tok = lambda w: pl.BlockSpec((1, T, w), lambda b, t: (b, t, 0))
    ins = [h, n1g, wgdn, wgz, wgate, cw]
    return pl.pallas_call(
        functools.partial(_gdn_proj_kernel, T=T),
        grid=(B, Lp // T),
        in_specs=[tok(D)] + [_const_spec(a.shape) for a in ins[1:]],
        out_specs=[tok(3 * GDN_W), tok(GDN_W), tok(2 * D_MODEL)],
        out_shape=[jax.ShapeDtypeStruct((B, Lp, 3 * GDN_W), f32),
                   jax.ShapeDtypeStruct((B, Lp, GDN_W), bf16),
                   jax.ShapeDtypeStruct((B, Lp, 2 * D_MODEL), bf16)],
        scratch_shapes=[pltpu.VMEM((SUBLANES, 3 * GDN_W), f32),
                        pltpu.VMEM((3, SUBLANES + T, GDN_W), f32)],
        compiler_params=pltpu.CompilerParams(
            dimension_semantics=("parallel", "arbitrary"), vmem_limit_bytes=VMEM_LIMIT),
        name="gdn_proj",
    )(*ins)


def _fox_attn_kernel(q_ref, k_ref, vt_ref, o_ref, m_s, acc_s, *, TQ, KB):
    qi = pl.program_id(2)
    pair = range(2)
    m_s[...] = jnp.full_like(m_s, NEG_INF)
    acc_s[...] = jnp.zeros_like(acc_s)
    qs = [q_ref[0, hh] for hh in pair]

    def scores(koff, c0):
        return [_mm_nt(k_ref[0, hh, pl.ds(koff, KB), :], qs[hh][c0:, :]) for hh in pair]

    def absorb(sT, koff, c0, masked):
        w = TQ - c0
        if masked:
            keep = (lax.broadcasted_iota(jnp.int32, (KB, w), 0) <= lax.broadcasted_iota(jnp.int32, (KB, w), 1))
            sT = [jnp.where(keep, s, NEG_INF) for s in sT]
        m_old = [m_s[hh, :, c0:] for hh in pair]
        m_new = [jnp.maximum(mo, jnp.max(s, axis=0, keepdims=True)) for mo, s in zip(m_old, sT)]
        pT = [jnp.exp2(s - mn).astype(bf16) for s, mn in zip(sT, m_new)]
        pv = [_mm(vt_ref[0, hh, :, pl.ds(koff, KB)], p) for hh, p in zip(pair, pT)]
        for hh in pair:
            acc_s[hh, :, c0:] = jnp.exp2(m_old[hh] - m_new[hh]) * acc_s[hh, :, c0:] + pv[hh]
            m_s[hh, :, c0:] = m_new[hh]

    nsub = TQ // KB

    def below_diagonal(first, n_tiles):
        offs = [pl.multiple_of(first * TQ + j * KB, KB) for j in range(n_tiles * nsub)]
        sTs = [scores(o, 0) for o in offs]
        for o, sT in zip(offs, sTs):
            absorb(sT, o, 0, False)

    def two_tiles(i, carry):
        below_diagonal(2 * i, 2)
        return carry

    def one_tile(i, carry):
        below_diagonal(qi - 1, 1)
        return carry

    lax.fori_loop(0, qi // 2, two_tiles, 0)
    lax.fori_loop(0, qi % 2, one_tile, 0)
    offs = [pl.multiple_of(qi * TQ + j * KB, KB) for j in range(nsub)]
    sTs = [scores(o, j * KB) for j, o in enumerate(offs)]
    for j, (o, sT) in enumerate(zip(offs, sTs)):
        absorb(sT, o, j * KB, True)

    outs = []
    for hh in pair:
        acc = acc_s[hh]
        outs.append(acc[0:FOX_HEAD_DIM, :] * (1.0 / acc[FOX_HEAD_DIM:FOX_HEAD_DIM + 1, :]))
    o_ref[0] = jnp.transpose(jnp.concatenate(outs, axis=0)).astype(bf16)


def _fox_attn(qa, ka, vt, *, TQ, KB):
    B, H, Lp, _ = qa.shape
    return pl.pallas_call(
        functools.partial(_fox_attn_kernel, TQ=TQ, KB=KB),
        grid=(B, H // 2, Lp // TQ),
        in_specs=[pl.BlockSpec((1, 2, TQ, LANES), lambda b, hp, qi: (b, hp, qi, 0)),
                  pl.BlockSpec((1, 2, Lp, LANES), lambda b, hp, qi: (b, hp, 0, 0)),
                  pl.BlockSpec((1, 2, FOX_VT_ROWS, Lp), lambda b, hp, qi: (b, hp, 0, 0))],
        out_specs=pl.BlockSpec((1, TQ, LANES), lambda b, hp, qi: (b, qi, hp)),
        out_shape=jax.ShapeDtypeStruct((B, Lp, FOX_W), bf16),
        scratch_shapes=[pltpu.VMEM((2, 1, TQ), f32),
                        pltpu.VMEM((2, FOX_VT_ROWS, TQ), f32)],
        compiler_params=pltpu.CompilerParams(
            dimension_semantics=("parallel", "parallel", "arbitrary"), vmem_limit_bytes=VMEM_LIMIT),
        name="fox_attn",
    )(qa, ka, vt)


def _gdn_kernel(x_ref, sm_ref, gz_ref, ng_ref, tribd_ref, onebd_ref, o_ref,
                S_ref, G_s, qg_s, qn_s, kn_s, kb_s, kd_s, vb_s, uh_s, w_s, aqk_s, *, T, C):
    t = pl.program_id(1)
    H, Dh = GDN_HEADS, GDN_HEAD_DIM

    @pl.when(t == 0)
    def _():
        S_ref[...] = jnp.zeros_like(S_ref)

    sm = sm_ref[0]
    lane = lax.broadcasted_iota(jnp.int32, (T, LANES), 1)
    gl = jnp.where(lane < _SM_A, 0.0, jnp.where(lane < _SM_A + H, sm, 0.0))
    G = _mm3(tribd_ref[...], gl)
    Glast = _mm3(onebd_ref[...], gl)
    G_s[...] = G
    eG = jnp.exp(G)
    eKd = jnp.exp(Glast - G)

    for h in range(H):
        qh = x_ref[0, :, h * Dh:(h + 1) * Dh]
        kh = x_ref[0, :, GDN_W + h * Dh:GDN_W + (h + 1) * Dh]
        vh = x_ref[0, :, 2 * GDN_W + h * Dh:2 * GDN_W + (h + 1) * Dh]
        beta = sm[:, _SM_B + h:_SM_B + h + 1]
        eg = eG[:, _SM_A + h:_SM_A + h + 1]
        ekd = eKd[:, _SM_A + h:_SM_A + h + 1]
        kbh = kh * beta
        qn_s[h] = qh.astype(bf16)
        qg_s[h] = (qh * eg).astype(bf16)
        kn_s[h] = kh.astype(bf16)
        kb_s[h] = kbh.astype(bf16)
        kd_s[h] = (kh * ekd).astype(bf16)
        vb_s[h, :, 0:Dh] = (vh * beta).astype(bf16)
        vb_s[h, :, Dh:2 * Dh] = (kbh * eg).astype(bf16)

    ri = lax.broadcasted_iota(jnp.int32, (C, C), 0)
    ci = lax.broadcasted_iota(jnp.int32, (C, C), 1)
    strict = ri > ci
    incl = ri >= ci
    eye = (ri == ci).astype(f32)
    n_double = int(np.log2(C)) - 1

    n_chunks = T // C
    GROUP = 3 if n_chunks % 3 == 0 else (2 if n_chunks % 2 == 0 else 1)
    heads = range(H)

    def phase_a(ci, carry):
        r0 = pl.multiple_of(ci * (GROUP * C), C)
        Gc = [G_s[pl.ds(r0 + cc * C, C), :] for cc in range(GROUP)]
        GcT = [jnp.transpose(g) for g in Gc]
        probs = [(cc, h) for cc in range(GROUP) for h in heads]
        rows = [pl.ds(r0 + cc * C, C) for cc, _ in probs]
        kn = [kn_s[h, r, :] for (_, h), r in zip(probs, rows)]
        kk = [_mm_nt(kb_s[h, r, :], k) for (_, h), r, k in zip(probs, rows, kn)]
        qk = [_mm_nt(qn_s[h, r, :], k) for (_, h), r, k in zip(probs, rows, kn)]
        dec = []
        for cc, h in probs:
            gcol = Gc[cc][:, _SM_A + h:_SM_A + h + 1]
            grow = GcT[cc][_SM_A + h:_SM_A + h + 1, :]
            dec.append(jnp.exp(jnp.where(incl, gcol - grow, NEG_INF)))
        for (_, h), r, a, d in zip(probs, rows, qk, dec):
            aqk_s[h, r, :] = jnp.where(incl, a * d, 0.0).astype(bf16)
        Nk = [jnp.where(strict, -(a * d), 0.0) for a, d in zip(kk, dec)]
        Tm = [eye + n for n in Nk]
        for _ in range(n_double):
            Nb = [n.astype(bf16) for n in Nk]
            Nk = [_mm(n, n) for n in Nb]
            Tm = [tm + _mm(tm.astype(bf16), n.astype(bf16)) for tm, n in zip(Tm, Nk)]
        for (_, h), r, tm in zip(probs, rows, Tm):
            uw = _mm(tm.astype(bf16), vb_s[h, r, :])
            uh_s[h, r, :] = uw[:, 0:Dh]
            w_s[h, r, :] = uw[:, Dh:2 * Dh].astype(bf16)
        return carry

    lax.fori_loop(0, T // (GROUP * C), phase_a, 0)

    def phase_b(c, carry):
        r0 = pl.multiple_of(c * C, C)
        rows = pl.ds(r0, C)
        Gc = G_s[rows, :]
        S = [S_ref[h] for h in heads]
        Sb = [s.astype(bf16) for s in S]
        wS = [_mm(w_s[h, rows, :], Sb[h]) for h in heads]
        qS = [_mm(qg_s[h, rows, :], Sb[h]) for h in heads]
        Ub = [(uh_s[h, rows, :] - wS[h]).astype(bf16) for h in heads]
        aU = [_mm(aqk_s[h, rows, :], Ub[h]) for h in heads]
        kU = [_mm_tn(kd_s[h, rows, :], Ub[h]) for h in heads]
        for h in heads:
            glast = Gc[C - 1:C, _SM_A + h:_SM_A + h + 1]
            S_ref[h] = S[h] * jnp.exp(glast) + kU[h]
        for h in heads:
            cs = slice(h * Dh, (h + 1) * Dh)
            on = _rms_rows(qS[h] + aU[h], ng_ref[:, cs])
            o_ref[0, rows, cs] = (on * gz_ref[0, rows, cs].astype(f32)).astype(bf16)
        return carry

    lax.fori_loop(0, T // C, phase_b, 0)


def _gdn_consts(T, C):
    n = T // C
    tribd = np.kron(np.eye(n, dtype=np.float32), np.tril(np.ones((C, C), np.float32)))
    onebd = np.kron(np.eye(n, dtype=np.float32), np.ones((C, C), np.float32))
    return jnp.asarray(tribd, bf16), jnp.asarray(onebd, bf16)


def _gdn(qkv, sm, gz, ng, consts, *, T, C):
    B, Lp, _ = qkv.shape
    H, Dh = GDN_HEADS, GDN_HEAD_DIM
    tribd, onebd = consts
    tok = lambda w: pl.BlockSpec((1, T, w), lambda b, t: (b, t, 0))
    ins = [qkv, sm, gz, ng, tribd, onebd]
    head_bf = pltpu.VMEM((H, T, Dh), bf16)
    return pl.pallas_call(
        functools.partial(_gdn_kernel, T=T, C=C),
        grid=(B, Lp // T),
        in_specs=[tok(3 * GDN_W), tok(LANES), tok(GDN_W)] + [_const_spec(a.shape) for a in ins[3:]],
        out_specs=tok(GDN_W),
        out_shape=jax.ShapeDtypeStruct((B, Lp, GDN_W), bf16),
        scratch_shapes=[pltpu.VMEM((H, Dh, Dh), f32),
                        pltpu.VMEM((T, LANES), f32),
                        head_bf, head_bf, head_bf, head_bf, head_bf,
                        pltpu.VMEM((H, T, 2 * Dh), bf16),
                        pltpu.VMEM((H, T, Dh), f32),
                        head_bf,
                        pltpu.VMEM((H, T, C), bf16)],
        compiler_params=pltpu.CompilerParams(
            dimension_semantics=("parallel", "arbitrary"), vmem_limit_bytes=VMEM_LIMIT),
        name="gdn",
    )(*ins)


def _merge_kernel(a_ref, b_ref, gate_ref, h_ref, wa_ref, wb_ref, wo_ref, n2g_ref, ho_ref, h2_ref, *, T, P):
    t = pl.program_id(1)
    ya = _mm(a_ref[0], wa_ref[...])
    yb = _mm(b_ref[0], wb_ref[...])
    g0 = gate_ref[0, :, 0:D_MODEL].astype(f32)
    g1 = gate_ref[0, :, D_MODEL:2 * D_MODEL].astype(f32)
    mixed = g0 * ya + g1 * yb
    hnew = h_ref[0] + _mm(mixed.astype(bf16), wo_ref[...])
    hnew = jnp.where(_row_valid(t, T, P), hnew, 0.0)
    ho_ref[0] = hnew
    h2_ref[0] = _rms_rows(hnew, n2g_ref[...]).astype(bf16)


def _merge(attn, ob, gates, h, wa, wb, wo, n2g, *, T, P):
    B, Lp, D = h.shape
    tok = lambda w: pl.BlockSpec((1, T, w), lambda b, t: (b, t, 0))
    ins = [attn, ob, gates, h, wa, wb, wo, n2g]
    return pl.pallas_call(
        functools.partial(_merge_kernel, T=T, P=P),
        grid=(B, Lp // T),
        in_specs=[tok(FOX_W), tok(GDN_W), tok(2 * D_MODEL), tok(D)] + [_const_spec(a.shape) for a in ins[4:]],
        out_specs=[tok(D), tok(D)],
        out_shape=[jax.ShapeDtypeStruct((B, Lp, D), f32), jax.ShapeDtypeStruct((B, Lp, D), bf16)],
        input_output_aliases={3: 0},
        compiler_params=pltpu.CompilerParams(
            dimension_semantics=("parallel", "parallel"), vmem_limit_bytes=VMEM_LIMIT),
        name="merge",
    )(*ins)


def _ffn_kernel(h2_ref, h_ref, wup_ref, cw_ref, wdn_ref, ho_ref, halo_ref, gbuf, vbuf, *, T, FC):
    t = pl.program_id(1)

    @pl.when(t == 0)
    def _():
        halo_ref[...] = jnp.zeros_like(halo_ref)

    x = h2_ref[0]
    acc = h_ref[0]

    def conv(buf, cs):
        y = cw_ref[FFN_CONV - 1:FFN_CONV, cs] * buf[SUBLANES:SUBLANES + T, :]
        for j in range(1, FFN_CONV):
            y = y + cw_ref[FFN_CONV - 1 - j:FFN_CONV - j, cs] * buf[SUBLANES - j:SUBLANES - j + T, :]
        return y

    for c in range(D_FF // FC):
        gs = slice(c * FC, (c + 1) * FC)
        vs = slice(D_FF + c * FC, D_FF + (c + 1) * FC)
        gbuf[0:SUBLANES, :] = halo_ref[:, gs]
        vbuf[0:SUBLANES, :] = halo_ref[:, vs]
        gbuf[SUBLANES:SUBLANES + T, :] = _mm(x, wup_ref[:, gs])
        vbuf[SUBLANES:SUBLANES + T, :] = _mm(x, wup_ref[:, vs])
        halo_ref[:, gs] = gbuf[T:T + SUBLANES, :]
        halo_ref[:, vs] = vbuf[T:T + SUBLANES, :]
        act = _silu(conv(gbuf, gs)) * conv(vbuf, vs)
        acc = acc + _mm(act.astype(bf16), wdn_ref[gs, :])
    ho_ref[0] = acc


def _ffn(h2, h, wup, cw, wdn, *, T):
    B, Lp, D = h.shape
    FC = FFN_COL_CHUNK
    tok = lambda w: pl.BlockSpec((1, T, w), lambda b, t: (b, t, 0))
    ins = [h2, h, wup, cw, wdn]
    return pl.pallas_call(
        functools.partial(_ffn_kernel, T=T, FC=FC),
        grid=(B, Lp // T),
        in_specs=[tok(D), tok(D)] + [_const_spec(a.shape) for a in ins[2:]],
        out_specs=tok(D),
        out_shape=jax.ShapeDtypeStruct((B, Lp, D), f32),
        scratch_shapes=[pltpu.VMEM((SUBLANES, 2 * D_FF), f32),
                        pltpu.VMEM((SUBLANES + T, FC), f32),
                        pltpu.VMEM((SUBLANES + T, FC), f32)],
        input_output_aliases={1: 0},
        compiler_params=pltpu.CompilerParams(
            dimension_semantics=("parallel", "arbitrary"), vmem_limit_bytes=VMEM_LIMIT),
        name="ffn",
    )(*ins)


def _prep_layer_params(norm1_g, w_in, fox_f_bias, fox_q_norm_g, fox_k_norm_g, gdn_conv_w, gdn_a_log,
                       gdn_dt_bias, gdn_norm_g, w_branch_a, w_branch_b, w_out, norm2_g, w_up,
                       ffn_conv_w, w_down):
    nl = w_in.shape[0]
    wsm = jnp.zeros((nl, D_MODEL, LANES), f32)
    wf = w_in[:, :, _OFF_FLOGIT:_OFF_FLOGIT + FOX_HEADS]
    for j in range(3):
        wsm = wsm.at[:, :, _SM_F + 8 * j:_SM_F + 8 * (j + 1)].set(wf)
    wsm = wsm.at[:, :, _SM_B:_SM_B + GDN_HEADS].set(w_in[:, :, _OFF_BLOGIT:_OFF_BLOGIT + GDN_HEADS])
    wsm = wsm.at[:, :, _SM_A:_SM_A + GDN_HEADS].set(w_in[:, :, _OFF_ALOGIT:_OFF_ALOGIT + GDN_HEADS])
    smb = jnp.zeros((nl, 1, LANES), f32)
    for j in range(3):
        smb = smb.at[:, 0, _SM_F + 8 * j:_SM_F + 8 * (j + 1)].set(fox_f_bias.astype(f32))
    smb = smb.at[:, 0, _SM_A:_SM_A + GDN_HEADS].set(gdn_dt_bias.astype(f32))
    alog = jnp.zeros((nl, 1, LANES), f32).at[:, 0, _SM_A:_SM_A + GDN_HEADS].set(gdn_a_log.astype(f32))
    row = lambda a: a.astype(f32)[:, None, :]
    return dict(
        n1g=row(norm1_g),
        wqk=w_in[:, :, _OFF_FQ:_OFF_FQ + 2 * FOX_W].astype(bf16),
        wvt=jnp.swapaxes(w_in[:, :, _OFF_FQ + 2 * FOX_W:_OFF_FQ + 3 * FOX_W], 1, 2).astype(bf16),
        wsm=wsm.astype(bf16), smb=smb, alog=alog,
        qg=row(jnp.tile(fox_q_norm_g, (1, FOX_HEADS))),
        kg=row(jnp.tile(fox_k_norm_g, (1, FOX_HEADS))),
        wgdn=w_in[:, :, _OFF_GQ:_OFF_GQ + 3 * GDN_W].astype(bf16),
        wgz=w_in[:, :, _OFF_GZ:_OFF_GZ + GDN_W].astype(bf16),
        wgate=w_in[:, :, _OFF_GATE:_OFF_GATE + 2 * D_MODEL].astype(bf16),
        gcw=gdn_conv_w.astype(f32),
        gng=row(jnp.tile(gdn_norm_g, (1, GDN_HEADS))),
        wa=w_branch_a.astype(bf16), wb=w_branch_b.astype(bf16), wo=w_out.astype(bf16),
        n2g=row(norm2_g),
        wup=w_up.astype(bf16), fcw=ffn_conv_w.astype(f32), wdn=w_down.astype(bf16),
    )


def _forward(x, meta_tokens, params, *, T):
    B, S, D = x.shape
    L = N_META + S
    Lp = -(-L // T) * T
    P = Lp - L
    meta = jnp.broadcast_to(meta_tokens.astype(x.dtype)[None], (B, N_META, D))
    h = jnp.concatenate([jnp.zeros((B, P, D), x.dtype), meta, x], axis=1)
    p = _prep_layer_params(*params)
    tp = _tile_plan(T)
    fox_consts = _fox_consts(tp["fox_proj"])
    gdn_consts = _gdn_consts(tp["gdn"], GDN_CHUNK)
    for l in range(p["wqk"].shape[0]):
        w = {k: v[l] for k, v in p.items()}
        qa, ka, vt, sm = _fox_proj(h, w["n1g"], w["wqk"], w["wvt"], w["wsm"], w["smb"], w["alog"], w["qg"],
                                   w["kg"], fox_consts, T=tp["fox_proj"], P=P)
        qkv, gz, gates = _gdn_proj(h, w["n1g"], w["wgdn"], w["wgz"], w["wgate"], w["gcw"], T=tp["gdn_proj"])
        attn = _fox_attn(qa, ka, vt, TQ=tp["attn"], KB=tp["key_block"])
        ob = _gdn(qkv, sm, gz, w["gng"], gdn_consts, T=tp["gdn"], C=GDN_CHUNK)
        h, h2 = _merge(attn, ob, gates, h, w["wa"], w["wb"], w["wo"], w["n2g"], T=tp["merge"], P=P)
        h = _ffn(h2, h, w["wup"], w["fcw"], w["wdn"], T=tp["ffn"])
    return h[:, P + N_META:]


def kernel(x, meta_tokens, norm1_g, w_in, fox_f_bias, fox_q_norm_g, fox_k_norm_g, gdn_conv_w, gdn_a_log,
           gdn_dt_bias, gdn_norm_g, w_branch_a, w_branch_b, w_out, norm2_g, w_up, ffn_conv_w, w_down):
    params = (norm1_g, w_in, fox_f_bias, fox_q_norm_g, fox_k_norm_g, gdn_conv_w, gdn_a_log, gdn_dt_bias,
              gdn_norm_g, w_branch_a, w_branch_b, w_out, norm2_g, w_up, ffn_conv_w, w_down)
    return _forward(x, meta_tokens, params, T=TOKEN_TILE)
```

```python
import functools

import numpy as np
import jax
import jax.numpy as jnp
from jax import lax
from jax.experimental import pallas as pl
from jax.experimental.pallas import tpu as pltpu

D_MODEL = 1024
DEPTH = 4
N_META = 16
EPS = 1e-6
NEG_INF = -1e30
FOX_HEADS = 8
FOX_HEAD_DIM = 64
FOX_W = FOX_HEADS * FOX_HEAD_DIM
GDN_HEADS = 8
GDN_HEAD_DIM = 128
GDN_CHUNK = 64
GDN_CONV = 4
GDN_W = GDN_HEADS * GDN_HEAD_DIM
D_FF = 2816
FFN_CONV = 3
FFN_COL_CHUNK = 2816

LANES = 128
SUBLANES = 8
MXU_DIM = 256
TOKEN_TILE = 3 * MXU_DIM
VMEM_LIMIT = 56 * 1024 * 1024
LOG2E = float(np.log2(np.e))
FOX_VT_ROWS = 80


def _tile_plan(align):
    half = align // 2
    return dict(fox_proj=align, gdn_proj=half, attn=align, key_block=align // 3 if align % 3 == 0 else half,
                gdn=half, merge=align, ffn=half)

_OFF_FQ = 0
_OFF_FLOGIT = 3 * FOX_W
_OFF_GQ = _OFF_FLOGIT + FOX_HEADS
_OFF_BLOGIT = _OFF_GQ + 3 * GDN_W
_OFF_ALOGIT = _OFF_BLOGIT + GDN_HEADS
_OFF_GZ = _OFF_ALOGIT + GDN_HEADS
_OFF_GATE = _OFF_GZ + GDN_W

_SM_F = 0
_SM_ONE = 24
_SM_B = 32
_SM_A = 40
_SM_MASK = 32

_AUG = FOX_HEAD_DIM

bf16 = jnp.bfloat16
f32 = jnp.float32


def _mm(a, b):
    return jnp.dot(a, b, preferred_element_type=f32)


def _mm_nt(a, b):
    return lax.dot_general(a, b, (((1,), (1,)), ((), ())), preferred_element_type=f32)


def _mm_tn(a, b):
    return lax.dot_general(a, b, (((0,), (0,)), ((), ())), preferred_element_type=f32)


def _split3(x):
    p1 = x.astype(bf16)
    r1 = x - p1.astype(f32)
    p2 = r1.astype(bf16)
    p3 = (r1 - p2.astype(f32)).astype(bf16)
    return p1, p2, p3


def _mm3(m01, x):
    p1, p2, p3 = _split3(x)
    return (_mm(m01, p3) + _mm(m01, p2)) + _mm(m01, p1)


def _sigmoid(x):
    return 1.0 / (1.0 + jnp.exp(-x))


def _silu(x):
    return x * _sigmoid(x)


def _softplus(x):
    return jnp.maximum(x, 0.0) + jnp.log1p(jnp.exp(-jnp.abs(x)))


def _rms_rows(x, g):
    ms = jnp.mean(x * x, axis=-1, keepdims=True)
    return x * lax.rsqrt(ms + EPS) * g


def _row_valid(t, T, P):
    pos = t * T + lax.broadcasted_iota(jnp.int32, (T, 1), 0)
    return pos >= P


def _fox_proj_kernel(h_ref, n1g_ref, wqk_ref, wvt_ref, wsm_ref, smb_ref, alog_ref, qg_ref, kg_ref,
                     tri_ref, blk_ref, eq_ref, ek_ref,
                     qa_ref, ka_ref, vt_ref, sm_ref, carry_ref, *, T, P):
    t = pl.program_id(1)

    @pl.when(t == 0)
    def _():
        carry_ref[...] = jnp.zeros_like(carry_ref)

    hn = _rms_rows(h_ref[0], n1g_ref[...]).astype(bf16)
    valid = _row_valid(t, T, P)
    lane = lax.broadcasted_iota(jnp.int32, (T, LANES), 1)

    ysm = _mm(hn, wsm_ref[...]) + smb_ref[...]
    logf = -_softplus(-ysm)
    logf = jnp.where(lane < _SM_ONE, jnp.where(valid, logf, 0.0), 0.0)
    F = _mm3(tri_ref[...], logf) + carry_ref[...]
    carry_ref[...] = F[T - 1:T, :]
    beta = _sigmoid(ysm)
    g = -jnp.exp(alog_ref[...]) * _softplus(ysm)
    sm_ref[0] = jnp.where(lane < _SM_B, 0.0,
                          jnp.where(lane < _SM_A, beta, jnp.where(lane < _SM_A + GDN_HEADS, g, 0.0)))

    F2 = F * LOG2E
    f1 = F2.astype(bf16).astype(f32)
    f2 = (F2 - f1).astype(bf16).astype(f32)
    f3 = (F2 - f1 - f2).astype(bf16).astype(f32)
    fsel = jnp.where(lane < 8, f1, jnp.where(lane < 16, f2, f3))
    key_bias = jnp.where(valid, 0.0, NEG_INF)
    pq = jnp.where(lane < _SM_ONE, fsel, jnp.where(lane < _SM_ONE + 8, 1.0, 0.0)).astype(bf16)
    pk = jnp.where(lane < _SM_ONE, -fsel,
                   jnp.where(lane < _SM_ONE + 8, 1.0,
                             jnp.where(lane < _SM_MASK + 8, key_bias, 0.0))).astype(bf16)

    yf = _mm(hn, wqk_ref[...])
    q = yf[:, :FOX_W]
    k = yf[:, FOX_W:]
    inv_d = 1.0 / FOX_HEAD_DIM
    qss = _mm((q * q).astype(bf16), blk_ref[...])
    kss = _mm((k * k).astype(bf16), blk_ref[...])
    qn = q * lax.rsqrt(qss * inv_d + EPS) * qg_ref[...] * (FOX_HEAD_DIM ** -0.5 * LOG2E)
    kn = k * lax.rsqrt(kss * inv_d + EPS) * kg_ref[...]

    aq = _mm(pq, eq_ref[FOX_W:, :])
    ak = _mm(pk, ek_ref[FOX_W:, :])

    def place(x, a):
        blocks = []
        for h in range(FOX_HEADS):
            src = x[:, (h // 2) * LANES:(h // 2 + 1) * LANES]
            if h % 2 == 1:
                src = pltpu.roll(src, FOX_HEAD_DIM, axis=1)
            blocks.append(jnp.where(lane < FOX_HEAD_DIM, src, a[:, h * LANES:(h + 1) * LANES]))
        return jnp.concatenate(blocks, axis=1)

    q_aug = place(qn, aq)
    k_aug = place(kn, ak)
    vt = _mm_nt(wvt_ref[...], hn)
    tail = jnp.where(lax.broadcasted_iota(jnp.int32, (FOX_VT_ROWS - FOX_HEAD_DIM, T), 0) == 0, 1.0, 0.0)
    for h in range(FOX_HEADS):
        sl = slice(h * LANES, (h + 1) * LANES)
        qa_ref[0, h] = q_aug[:, sl].astype(bf16)
        ka_ref[0, h] = k_aug[:, sl].astype(bf16)
        vt_ref[0, h, 0:FOX_HEAD_DIM, :] = vt[h * FOX_HEAD_DIM:(h + 1) * FOX_HEAD_DIM, :].astype(bf16)
        vt_ref[0, h, FOX_HEAD_DIM:FOX_VT_ROWS, :] = tail.astype(bf16)


def _fox_consts(T):
    tri = np.tril(np.ones((T, T), np.float32))
    blk = np.kron(np.eye(FOX_HEADS, dtype=np.float32), np.ones((FOX_HEAD_DIM, FOX_HEAD_DIM), np.float32))
    eq = np.zeros((FOX_W + LANES, FOX_HEADS * LANES), np.float32)
    ek = np.zeros_like(eq)
    for h in range(FOX_HEADS):
        base = h * LANES
        for d in range(FOX_HEAD_DIM):
            eq[h * FOX_HEAD_DIM + d, base + d] = 1.0
            ek[h * FOX_HEAD_DIM + d, base + d] = 1.0
        for j in range(3):
            eq[FOX_W + _SM_F + 8 * j + h, base + _AUG + j] = 1.0
            eq[FOX_W + _SM_ONE + h, base + _AUG + 3 + j] = 1.0
            ek[FOX_W + _SM_ONE + h, base + _AUG + j] = 1.0
            ek[FOX_W + _SM_F + 8 * j + h, base + _AUG + 3 + j] = 1.0
        eq[FOX_W + _SM_ONE + h, base + _AUG + 6] = 1.0
        ek[FOX_W + _SM_MASK + h, base + _AUG + 6] = 1.0
    as_bf = lambda a: jnp.asarray(a, bf16)
    return as_bf(tri), as_bf(blk), as_bf(eq), as_bf(ek)


def _const_spec(shape):
    nd = len(shape)
    return pl.BlockSpec(shape, lambda *_: (0,) * nd, pipeline_mode=pl.Buffered(1))


def _fox_proj(h, n1g, wqk, wvt, wsm, smb, alog, qg, kg, consts, *, T, P):
    B, Lp, D = h.shape
    nT = Lp // T
    tri, blk, eq, ek = consts
    head_spec = pl.BlockSpec((1, FOX_HEADS, T, LANES), lambda b, t: (b, 0, t, 0))
    head_shape = jax.ShapeDtypeStruct((B, FOX_HEADS, Lp, LANES), bf16)
    ins = [h, n1g, wqk, wvt, wsm, smb, alog, qg, kg, tri, blk, eq, ek]
    in_specs = [pl.BlockSpec((1, T, D), lambda b, t: (b, t, 0))] + [_const_spec(a.shape) for a in ins[1:]]
    return pl.pallas_call(
        functools.partial(_fox_proj_kernel, T=T, P=P),
        grid=(B, nT),
        in_specs=in_specs,
        out_specs=[head_spec, head_spec,
                   pl.BlockSpec((1, FOX_HEADS, FOX_VT_ROWS, T), lambda b, t: (b, 0, 0, t)),
                   pl.BlockSpec((1, T, LANES), lambda b, t: (b, t, 0))],
        out_shape=[head_shape, head_shape,
                   jax.ShapeDtypeStruct((B, FOX_HEADS, FOX_VT_ROWS, Lp), bf16),
                   jax.ShapeDtypeStruct((B, Lp, LANES), f32)],
        scratch_shapes=[pltpu.VMEM((1, LANES), f32)],
        compiler_params=pltpu.CompilerParams(
            dimension_semantics=("parallel", "arbitrary"), vmem_limit_bytes=VMEM_LIMIT),
        name="fox_proj",
    )(*ins)


def _gdn_proj_kernel(h_ref, n1g_ref, wgdn_ref, wgz_ref, wgate_ref, cw_ref, qkv_ref, gz_ref, gate_ref,
                     halo_ref, xbuf, *, T):
    t = pl.program_id(1)
    Dh = GDN_HEAD_DIM

    @pl.when(t == 0)
    def _():
        halo_ref[...] = jnp.zeros_like(halo_ref)

    hn = _rms_rows(h_ref[0], n1g_ref[...]).astype(bf16)
    for c in range(3):
        sl = slice(c * GDN_W, (c + 1) * GDN_W)
        xbuf[c, 0:SUBLANES, :] = halo_ref[:, sl]
        xbuf[c, SUBLANES:SUBLANES + T, :] = _mm(hn, wgdn_ref[:, sl])
        halo_ref[:, sl] = xbuf[c, T:T + SUBLANES, :]

    def other_projection(c):
        if c == 0:
            gz_ref[0] = _silu(_mm(hn, wgz_ref[...])).astype(bf16)
        else:
            sl = slice((c - 1) * D_MODEL, c * D_MODEL)
            gate_ref[0, :, sl] = _sigmoid(_mm(hn, wgate_ref[:, sl])).astype(bf16)

    for c in range(3):
        sl = slice(c * GDN_W, (c + 1) * GDN_W)
        other_projection(c)
        y = cw_ref[GDN_CONV - 1:GDN_CONV, sl] * xbuf[c, SUBLANES:SUBLANES + T, :]
        for j in range(1, GDN_CONV):
            y = y + cw_ref[GDN_CONV - 1 - j:GDN_CONV - j, sl] * xbuf[c, SUBLANES - j:SUBLANES - j + T, :]
        y = _silu(y)
        if c == 2:
            qkv_ref[0, :, sl] = y
        else:
            for h in range(GDN_HEADS):
                a = y[:, h * Dh:(h + 1) * Dh]
                a = a * lax.rsqrt(jnp.sum(a * a, axis=-1, keepdims=True) + EPS)
                if c == 0:
                    a = a * (Dh ** -0.5)
                qkv_ref[0, :, c * GDN_W + h * Dh:c * GDN_W + (h + 1) * Dh] = a


def _gdn_proj(h, n1g, wgdn, wgz, wgate, cw, *, T):
    B, Lp, D = h.shape
    tok = lambda w: pl.BlockSpec((1, T, w), lambda b, t: (b, t, 0))
    ins = [h, n1g, wgdn, wgz, wgate, cw]
    return pl.pallas_call(
        functools.partial(_gdn_proj_kernel, T=T),
        grid=(B, Lp // T),
        in_specs=[tok(D)] + [_const_spec(a.shape) for a in ins[1:]],
        out_specs=[tok(3 * GDN_W), tok(GDN_W), tok(2 * D_MODEL)],
        out_shape=[jax.ShapeDtypeStruct((B, Lp, 3 * GDN_W), f32),
                   jax.ShapeDtypeStruct((B, Lp, GDN_W), bf16),
                   jax.ShapeDtypeStruct((B, Lp, 2 * D_MODEL), bf16)],
        scratch_shapes=[pltpu.VMEM((SUBLANES, 3 * GDN_W), f32),
                        pltpu.VMEM((3, SUBLANES + T, GDN_W), f32)],
        compiler_params=pltpu.CompilerParams(
            dimension_semantics=("parallel", "arbitrary"), vmem_limit_bytes=VMEM_LIMIT),
        name="gdn_proj",
    )(*ins)


def _fox_attn_kernel(q_ref, k_ref, vt_ref, o_ref, m_s, acc_s, *, TQ, KB):
    qi = pl.program_id(2)
    pair = range(2)
    m_s[...] = jnp.full_like(m_s, NEG_INF)
    acc_s[...] = jnp.zeros_like(acc_s)
    qs = [q_ref[0, hh] for hh in pair]

    def scores(koff, c0):
        return [_mm_nt(k_ref[0, hh, pl.ds(koff, KB), :], qs[hh][c0:, :]) for hh in pair]

    def absorb(sT, koff, c0, masked):
        w = TQ - c0
        if masked:
            keep = (lax.broadcasted_iota(jnp.int32, (KB, w), 0) <= lax.broadcasted_iota(jnp.int32, (KB, w), 1))
            sT = [jnp.where(keep, s, NEG_INF) for s in sT]
        m_old = [m_s[hh, :, c0:] for hh in pair]
        m_new = [jnp.maximum(mo, jnp.max(s, axis=0, keepdims=True)) for mo, s in zip(m_old, sT)]
        pT = [jnp.exp2(s - mn).astype(bf16) for s, mn in zip(sT, m_new)]
        pv = [_mm(vt_ref[0, hh, :, pl.ds(koff, KB)], p) for hh, p in zip(pair, pT)]
        for hh in pair:
            acc_s[hh, :, c0:] = jnp.exp2(m_old[hh] - m_new[hh]) * acc_s[hh, :, c0:] + pv[hh]
            m_s[hh, :, c0:] = m_new[hh]

    nsub = TQ // KB

    def below_diagonal(first, n_tiles):
        offs = [pl.multiple_of(first * TQ + j * KB, KB) for j in range(n_tiles * nsub)]
        sTs = [scores(o, 0) for o in offs]
        for o, sT in zip(offs, sTs):
            absorb(sT, o, 0, False)

    def two_tiles(i, carry):
        below_diagonal(2 * i, 2)
        return carry

    def one_tile(i, carry):
        below_diagonal(qi - 1, 1)
        return carry

    lax.fori_loop(0, qi // 2, two_tiles, 0)
    lax.fori_loop(0, qi % 2, one_tile, 0)
    offs = [pl.multiple_of(qi * TQ + j * KB, KB) for j in range(nsub)]
    sTs = [scores(o, j * KB) for j, o in enumerate(offs)]
    for j, (o, sT) in enumerate(zip(offs, sTs)):
        absorb(sT, o, j * KB, True)

    outs = []
    for hh in pair:
        acc = acc_s[hh]
        outs.append(acc[0:FOX_HEAD_DIM, :] * (1.0 / acc[FOX_HEAD_DIM:FOX_HEAD_DIM + 1, :]))
    o_ref[0] = jnp.transpose(jnp.concatenate(outs, axis=0)).astype(bf16)


def _fox_attn(qa, ka, vt, *, TQ, KB):
    B, H, Lp, _ = qa.shape
    return pl.pallas_call(
        functools.partial(_fox_attn_kernel, TQ=TQ, KB=KB),
        grid=(B, H // 2, Lp // TQ),
        in_specs=[pl.BlockSpec((1, 2, TQ, LANES), lambda b, hp, qi: (b, hp, qi, 0)),
                  pl.BlockSpec((1, 2, Lp, LANES), lambda b, hp, qi: (b, hp, 0, 0)),
                  pl.BlockSpec((1, 2, FOX_VT_ROWS, Lp), lambda b, hp, qi: (b, hp, 0, 0))],
        out_specs=pl.BlockSpec((1, TQ, LANES), lambda b, hp, qi: (b, qi, hp)),
        out_shape=jax.ShapeDtypeStruct((B, Lp, FOX_W), bf16),
        scratch_shapes=[pltpu.VMEM((2, 1, TQ), f32),
                        pltpu.VMEM((2, FOX_VT_ROWS, TQ), f32)],
        compiler_params=pltpu.CompilerParams(
            dimension_semantics=("parallel", "parallel", "arbitrary"), vmem_limit_bytes=VMEM_LIMIT),
        name="fox_attn",
    )(qa, ka, vt)


def _gdn_kernel(x_ref, sm_ref, gz_ref, ng_ref, tribd_ref, onebd_ref, o_ref,
                S_ref, G_s, qg_s, qn_s, kn_s, kb_s, kd_s, vb_s, uh_s, w_s, aqk_s, *, T, C):
    t = pl.program_id(1)
    H, Dh = GDN_HEADS, GDN_HEAD_DIM

    @pl.when(t == 0)
    def _():
        S_ref[...] = jnp.zeros_like(S_ref)

    sm = sm_ref[0]
    lane = lax.broadcasted_iota(jnp.int32, (T, LANES), 1)
    gl = jnp.where(lane < _SM_A, 0.0, jnp.where(lane < _SM_A + H, sm, 0.0))
    G = _mm3(tribd_ref[...], gl)
    Glast = _mm3(onebd_ref[...], gl)
    G_s[...] = G
    eG = jnp.exp(G)
    eKd = jnp.exp(Glast - G)

    for h in range(H):
        qh = x_ref[0, :, h * Dh:(h + 1) * Dh]
        kh = x_ref[0, :, GDN_W + h * Dh:GDN_W + (h + 1) * Dh]
        vh = x_ref[0, :, 2 * GDN_W + h * Dh:2 * GDN_W + (h + 1) * Dh]
        beta = sm[:, _SM_B + h:_SM_B + h + 1]
        eg = eG[:, _SM_A + h:_SM_A + h + 1]
        ekd = eKd[:, _SM_A + h:_SM_A + h + 1]
        kbh = kh * beta
        qn_s[h] = qh.astype(bf16)
        qg_s[h] = (qh * eg).astype(bf16)
        kn_s[h] = kh.astype(bf16)
        kb_s[h] = kbh.astype(bf16)
        kd_s[h] = (kh * ekd).astype(bf16)
        vb_s[h, :, 0:Dh] = (vh * beta).astype(bf16)
        vb_s[h, :, Dh:2 * Dh] = (kbh * eg).astype(bf16)

    ri = lax.broadcasted_iota(jnp.int32, (C, C), 0)
    ci = lax.broadcasted_iota(jnp.int32, (C, C), 1)
    strict = ri > ci
    incl = ri >= ci
    eye = (ri == ci).astype(f32)
    n_double = int(np.log2(C)) - 1

    n_chunks = T // C
    GROUP = 3 if n_chunks % 3 == 0 else (2 if n_chunks % 2 == 0 else 1)
    heads = range(H)

    def phase_a(ci, carry):
        r0 = pl.multiple_of(ci * (GROUP * C), C)
        Gc = [G_s[pl.ds(r0 + cc * C, C), :] for cc in range(GROUP)]
        GcT = [jnp.transpose(g) for g in Gc]
        probs = [(cc, h) for cc in range(GROUP) for h in heads]
        rows = [pl.ds(r0 + cc * C, C) for cc, _ in probs]
        kn = [kn_s[h, r, :] for (_, h), r in zip(probs, rows)]
        kk = [_mm_nt(kb_s[h, r, :], k) for (_, h), r, k in zip(probs, rows, kn)]
        qk = [_mm_nt(qn_s[h, r, :], k) for (_, h), r, k in zip(probs, rows, kn)]
        dec = []
        for cc, h in probs:
            gcol = Gc[cc][:, _SM_A + h:_SM_A + h + 1]
            grow = GcT[cc][_SM_A + h:_SM_A + h + 1, :]
            dec.append(jnp.exp(jnp.where(incl, gcol - grow, NEG_INF)))
        for (_, h), r, a, d in zip(probs, rows, qk, dec):
            aqk_s[h, r, :] = jnp.where(incl, a * d, 0.0).astype(bf16)
        Nk = [jnp.where(strict, -(a * d), 0.0) for a, d in zip(kk, dec)]
        Tm = [eye + n for n in Nk]
        for _ in range(n_double):
            Nb = [n.astype(bf16) for n in Nk]
            Nk = [_mm(n, n) for n in Nb]
            Tm = [tm + _mm(tm.astype(bf16), n.astype(bf16)) for tm, n in zip(Tm, Nk)]
        for (_, h), r, tm in zip(probs, rows, Tm):
            uw = _mm(tm.astype(bf16), vb_s[h, r, :])
            uh_s[h, r, :] = uw[:, 0:Dh]
            w_s[h, r, :] = uw[:, Dh:2 * Dh].astype(bf16)
        return carry

    lax.fori_loop(0, T // (GROUP * C), phase_a, 0)

    def phase_b(c, carry):
        r0 = pl.multiple_of(c * C, C)
        rows = pl.ds(r0, C)
        Gc = G_s[rows, :]
        S = [S_ref[h] for h in heads]
        Sb = [s.astype(bf16) for s in S]
        wS = [_mm(w_s[h, rows, :], Sb[h]) for h in heads]
        qS = [_mm(qg_s[h, rows, :], Sb[h]) for h in heads]
        Ub = [(uh_s[h, rows, :] - wS[h]).astype(bf16) for h in heads]
        aU = [_mm(aqk_s[h, rows, :], Ub[h]) for h in heads]
        kU = [_mm_tn(kd_s[h, rows, :], Ub[h]) for h in heads]
        for h in heads:
            glast = Gc[C - 1:C, _SM_A + h:_SM_A + h + 1]
            S_ref[h] = S[h] * jnp.exp(glast) + kU[h]
        for h in heads:
            cs = slice(h * Dh, (h + 1) * Dh)
            on = _rms_rows(qS[h] + aU[h], ng_ref[:, cs])
            o_ref[0, rows, cs] = (on * gz_ref[0, rows, cs].astype(f32)).astype(bf16)
        return carry

    lax.fori_loop(0, T // C, phase_b, 0)


def _gdn_consts(T, C):
    n = T // C
    tribd = np.kron(np.eye(n, dtype=np.float32), np.tril(np.ones((C, C), np.float32)))
    onebd = np.kron(np.eye(n, dtype=np.float32), np.ones((C, C), np.float32))
    return jnp.asarray(tribd, bf16), jnp.asarray(onebd, bf16)


def _gdn(qkv, sm, gz, ng, consts, *, T, C):
    B, Lp, _ = qkv.shape
    H, Dh = GDN_HEADS, GDN_HEAD_DIM
    tribd, onebd = consts
    tok = lambda w: pl.BlockSpec((1, T, w), lambda b, t: (b, t, 0))
    ins = [qkv, sm, gz, ng, tribd, onebd]
    head_bf = pltpu.VMEM((H, T, Dh), bf16)
    return pl.pallas_call(
        functools.partial(_gdn_kernel, T=T, C=C),
        grid=(B, Lp // T),
        in_specs=[tok(3 * GDN_W), tok(LANES), tok(GDN_W)] + [_const_spec(a.shape) for a in ins[3:]],
        out_specs=tok(GDN_W),
        out_shape=jax.ShapeDtypeStruct((B, Lp, GDN_W), bf16),
        scratch_shapes=[pltpu.VMEM((H, Dh, Dh), f32),
                        pltpu.VMEM((T, LANES), f32),
                        head_bf, head_bf, head_bf, head_bf, head_bf,
                        pltpu.VMEM((H, T, 2 * Dh), bf16),
                        pltpu.VMEM((H, T, Dh), f32),
                        head_bf,
                        pltpu.VMEM((H, T, C), bf16)],
        compiler_params=pltpu.CompilerParams(
            dimension_semantics=("parallel", "arbitrary"), vmem_limit_bytes=VMEM_LIMIT),
        name="gdn",
    )(*ins)


def _merge_kernel(a_ref, b_ref, gate_ref, h_ref, wa_ref, wb_ref, wo_ref, n2g_ref, ho_ref, h2_ref, *, T, P):
    t = pl.program_id(1)
    ya = _mm(a_ref[0], wa_ref[...])
    yb = _mm(b_ref[0], wb_ref[...])
    g0 = gate_ref[0, :, 0:D_MODEL].astype(f32)
    g1 = gate_ref[0, :, D_MODEL:2 * D_MODEL].astype(f32)
    mixed = g0 * ya + g1 * yb
    hnew = h_ref[0] + _mm(mixed.astype(bf16), wo_ref[...])
    hnew = jnp.where(_row_valid(t, T, P), hnew, 0.0)
    ho_ref[0] = hnew
    h2_ref[0] = _rms_rows(hnew, n2g_ref[...]).astype(bf16)


def _merge(attn, ob, gates, h, wa, wb, wo, n2g, *, T, P):
    B, Lp, D = h.shape
    tok = lambda w: pl.BlockSpec((1, T, w), lambda b, t: (b, t, 0))
    ins = [attn, ob, gates, h, wa, wb, wo, n2g]
    return pl.pallas_call(
        functools.partial(_merge_kernel, T=T, P=P),
        grid=(B, Lp // T),
        in_specs=[tok(FOX_W), tok(GDN_W), tok(2 * D_MODEL), tok(D)] + [_const_spec(a.shape) for a in ins[4:]],
        out_specs=[tok(D), tok(D)],
        out_shape=[jax.ShapeDtypeStruct((B, Lp, D), f32), jax.ShapeDtypeStruct((B, Lp, D), bf16)],
        input_output_aliases={3: 0},
        compiler_params=pltpu.CompilerParams(
            dimension_semantics=("parallel", "parallel"), vmem_limit_bytes=VMEM_LIMIT),
        name="merge",
    )(*ins)


def _ffn_kernel(h2_ref, h_ref, wup_ref, cw_ref, wdn_ref, ho_ref, halo_ref, gbuf, vbuf, *, T, FC):
    t = pl.program_id(1)

    @pl.when(t == 0)
    def _():
        halo_ref[...] = jnp.zeros_like(halo_ref)

    x = h2_ref[0]
    acc = h_ref[0]

    def conv(buf, cs):
        y = cw_ref[FFN_CONV - 1:FFN_CONV, cs] * buf[SUBLANES:SUBLANES + T, :]
        for j in range(1, FFN_CONV):
            y = y + cw_ref[FFN_CONV - 1 - j:FFN_CONV - j, cs] * buf[SUBLANES - j:SUBLANES - j + T, :]
        return y

    for c in range(D_FF // FC):
        gs = slice(c * FC, (c + 1) * FC)
        vs = slice(D_FF + c * FC, D_FF + (c + 1) * FC)
        gbuf[0:SUBLANES, :] = halo_ref[:, gs]
        vbuf[0:SUBLANES, :] = halo_ref[:, vs]
        gbuf[SUBLANES:SUBLANES + T, :] = _mm(x, wup_ref[:, gs])
        vbuf[SUBLANES:SUBLANES + T, :] = _mm(x, wup_ref[:, vs])
        halo_ref[:, gs] = gbuf[T:T + SUBLANES, :]
        halo_ref[:, vs] = vbuf[T:T + SUBLANES, :]
        act = _silu(conv(gbuf, gs)) * conv(vbuf, vs)
        acc = acc + _mm(act.astype(bf16), wdn_ref[gs, :])
    ho_ref[0] = acc


def _ffn(h2, h, wup, cw, wdn, *, T):
    B, Lp, D = h.shape
    FC = FFN_COL_CHUNK
    tok = lambda w: pl.BlockSpec((1, T, w), lambda b, t: (b, t, 0))
    ins = [h2, h, wup, cw, wdn]
    return pl.pallas_call(
        functools.partial(_ffn_kernel, T=T, FC=FC),
        grid=(B, Lp // T),
        in_specs=[tok(D), tok(D)] + [_const_spec(a.shape) for a in ins[2:]],
        out_specs=tok(D),
        out_shape=jax.ShapeDtypeStruct((B, Lp, D), f32),
        scratch_shapes=[pltpu.VMEM((SUBLANES, 2 * D_FF), f32),
                        pltpu.VMEM((SUBLANES + T, FC), f32),
                        pltpu.VMEM((SUBLANES + T, FC), f32)],
        input_output_aliases={1: 0},
        compiler_params=pltpu.CompilerParams(
            dimension_semantics=("parallel", "arbitrary"), vmem_limit_bytes=VMEM_LIMIT),
        name="ffn",
    )(*ins)


def _prep_layer_params(norm1_g, w_in, fox_f_bias, fox_q_norm_g, fox_k_norm_g, gdn_conv_w, gdn_a_log,
                       gdn_dt_bias, gdn_norm_g, w_branch_a, w_branch_b, w_out, norm2_g, w_up,
                       ffn_conv_w, w_down):
    nl = w_in.shape[0]
    wsm = jnp.zeros((nl, D_MODEL, LANES), f32)
    wf = w_in[:, :, _OFF_FLOGIT:_OFF_FLOGIT + FOX_HEADS]
    for j in range(3):
        wsm = wsm.at[:, :, _SM_F + 8 * j:_SM_F + 8 * (j + 1)].set(wf)
    wsm = wsm.at[:, :, _SM_B:_SM_B + GDN_HEADS].set(w_in[:, :, _OFF_BLOGIT:_OFF_BLOGIT + GDN_HEADS])
    wsm = wsm.at[:, :, _SM_A:_SM_A + GDN_HEADS].set(w_in[:, :, _OFF_ALOGIT:_OFF_ALOGIT + GDN_HEADS])
    smb = jnp.zeros((nl, 1, LANES), f32)
    for j in range(3):
        smb = smb.at[:, 0, _SM_F + 8 * j:_SM_F + 8 * (j + 1)].set(fox_f_bias.astype(f32))
    smb = smb.at[:, 0, _SM_A:_SM_A + GDN_HEADS].set(gdn_dt_bias.astype(f32))
    alog = jnp.zeros((nl, 1, LANES), f32).at[:, 0, _SM_A:_SM_A + GDN_HEADS].set(gdn_a_log.astype(f32))
    row = lambda a: a.astype(f32)[:, None, :]
    return dict(
        n1g=row(norm1_g),
        wqk=w_in[:, :, _OFF_FQ:_OFF_FQ + 2 * FOX_W].astype(bf16),
        wvt=jnp.swapaxes(w_in[:, :, _OFF_FQ + 2 * FOX_W:_OFF_FQ + 3 * FOX_W], 1, 2).astype(bf16),
        wsm=wsm.astype(bf16), smb=smb, alog=alog,
        qg=row(jnp.tile(fox_q_norm_g, (1, FOX_HEADS))),
        kg=row(jnp.tile(fox_k_norm_g, (1, FOX_HEADS))),
        wgdn=w_in[:, :, _OFF_GQ:_OFF_GQ + 3 * GDN_W].astype(bf16),
        wgz=w_in[:, :, _OFF_GZ:_OFF_GZ + GDN_W].astype(bf16),
        wgate=w_in[:, :, _OFF_GATE:_OFF_GATE + 2 * D_MODEL].astype(bf16),
        gcw=gdn_conv_w.astype(f32),
        gng=row(jnp.tile(gdn_norm_g, (1, GDN_HEADS))),
        wa=w_branch_a.astype(bf16), wb=w_branch_b.astype(bf16), wo=w_out.astype(bf16),
        n2g=row(norm2_g),
        wup=w_up.astype(bf16), fcw=ffn_conv_w.astype(f32), wdn=w_down.astype(bf16),
    )


def _forward(x, meta_tokens, params, *, T):
    B, S, D = x.shape
    L = N_META + S
    Lp = -(-L // T) * T
    P = Lp - L
    meta = jnp.broadcast_to(meta_tokens.astype(x.dtype)[None], (B, N_META, D))
    h = jnp.concatenate([jnp.zeros((B, P, D), x.dtype), meta, x], axis=1)
    p = _prep_layer_params(*params)
    tp = _tile_plan(T)
    fox_consts = _fox_consts(tp["fox_proj"])
    gdn_consts = _gdn_consts(tp["gdn"], GDN_CHUNK)
    for l in range(p["wqk"].shape[0]):
        w = {k: v[l] for k, v in p.items()}
        qa, ka, vt, sm = _fox_proj(h, w["n1g"], w["wqk"], w["wvt"], w["wsm"], w["smb"], w["alog"], w["qg"],
                                   w["kg"], fox_consts, T=tp["fox_proj"], P=P)
        qkv, gz, gates = _gdn_proj(h, w["n1g"], w["wgdn"], w["wgz"], w["wgate"], w["gcw"], T=tp["gdn_proj"])
        attn = _fox_attn(qa, ka, vt, TQ=tp["attn"], KB=tp["key_block"])
        ob = _gdn(qkv, sm, gz, w["gng"], gdn_consts, T=tp["gdn"], C=GDN_CHUNK)
        h, h2 = _merge(attn, ob, gates, h, w["wa"], w["wb"], w["wo"], w["n2g"], T=tp["merge"], P=P)
        h = _ffn(h2, h, w["wup"], w["fcw"], w["wdn"], T=tp["ffn"])
    return h[:, P + N_META:]


def kernel(x, meta_tokens, norm1_g, w_in, fox_f_bias, fox_q_norm_g, fox_k_norm_g, gdn_conv_w, gdn_a_log,
           gdn_dt_bias, gdn_norm_g, w_branch_a, w_branch_b, w_out, norm2_g, w_up, ffn_conv_w, w_down):
    params = (norm1_g, w_in, fox_f_bias, fox_q_norm_g, fox_k_norm_g, gdn_conv_w, gdn_a_log, gdn_dt_bias,
              gdn_norm_g, w_branch_a, w_branch_b, w_out, norm2_g, w_up, ffn_conv_w, w_down)
    return _forward(x, meta_tokens, params, T=TOKEN_TILE)
```
